```python
import math
import jax, jax.numpy as jnp
from jax import lax
import numpy as np

D_MODEL = 1024
BATCH = 4
SEQ = 4096
DEPTH = 4

CONV_WIDTH = D_MODEL
CONV_K = 3
POOL_WIDTH = D_MODEL
POOL_GROUPS = 4
POOL_WINDOWS = (2, 4, 8, 16)
N_HEADS = 16
N_KV_HEADS = 4
HEAD_DIM = D_MODEL // N_HEADS
WINDOW = 128
BLOCK = 128
N_BUCKETS = 32
MAX_DISTANCE = 128
N_BRANCHES = 3
D_FF = -(-8 * D_MODEL // (3 * 256)) * 256

EPS = 1e-6
NEG_INF = -1e30

Q_WIDTH = N_HEADS * HEAD_DIM
KV_WIDTH = N_KV_HEADS * HEAD_DIM
IN_SIZES = (CONV_WIDTH, CONV_WIDTH, CONV_WIDTH, POOL_WIDTH, Q_WIDTH, KV_WIDTH, KV_WIDTH,
            D_MODEL, D_MODEL, D_MODEL)
IN_TOTAL = sum(IN_SIZES)

kernel_name = "hybrid_conv_pool_swa_encoder"


def rms_norm(x, g):
    xf = x.astype(jnp.float32)
    y = xf * lax.rsqrt(jnp.mean(xf * xf, axis=-1, keepdims=True) + EPS)
    return (y * g.astype(jnp.float32)).astype(x.dtype)


def split_points():
    pts, acc = [], 0
    for s in IN_SIZES[:-1]:
        acc += s
        pts.append(acc)
    return pts


def t5_bucket(rel):
    half = N_BUCKETS // 2
    max_exact = half // 2
    ret = jnp.where(rel > 0, half, 0)
    n = jnp.abs(rel)
    nf = jnp.maximum(n, 1).astype(jnp.float32)
    large = max_exact + (jnp.log(nf / max_exact) / math.log(MAX_DISTANCE / max_exact)
                         * (half - max_exact)).astype(jnp.int32)
    large = jnp.minimum(large, half - 1)
    return ret + jnp.where(n < max_exact, n, large)


def short_conv_mixer(b_gate, c_gate, xin, conv_w, w_out):
    u = c_gate * xin
    y = lax.conv_general_dilated(u, conv_w, window_strides=(1,),
                                 padding=[(CONV_K // 2, CONV_K // 2)],
                                 dimension_numbers=("NWC", "WIO", "NWC"),
                                 feature_group_count=u.shape[-1])
    return (b_gate * y) @ w_out


def multiscale_pool_mixer(u, w_pool, pool_scale):
    B, S, W = u.shape
    cg = W // POOL_GROUPS
    uf = u.astype(jnp.float32).reshape(B, S, POOL_GROUPS, cg)
    cs = jnp.pad(jnp.cumsum(uf, axis=1), ((0, 0), (1, 0), (0, 0), (0, 0)))
    t = jnp.arange(S)
    outs = []
    for gi, w in enumerate(POOL_WINDOWS):
        lo = jnp.maximum(t - w // 2, 0)
        hi = jnp.minimum(t + (w - 1 - w // 2), S - 1)
        csg = cs[:, :, gi]
        s = jnp.take(csg, hi + 1, axis=1) - jnp.take(csg, lo, axis=1)
        cnt = (hi - lo + 1).astype(jnp.float32)[None, :, None]
        outs.append(s / cnt - uf[:, :, gi])
    p = jnp.stack(outs, axis=2).astype(u.dtype)
    y = jnp.einsum("bsgc,gcd->bsgd", p, w_pool).reshape(B, S, W)
    return y * pool_scale


def windowed_gqa(q, k, v, rel_bias, sink):
    B, S, _ = q.shape
    nb = S // BLOCK
    G = N_HEADS // N_KV_HEADS
    qb = q.reshape(B, nb, BLOCK, N_KV_HEADS, G, HEAD_DIM)
    pad = ((0, 0), (BLOCK, BLOCK), (0, 0))
    kp = jnp.pad(k, pad).reshape(B, nb + 2, BLOCK, N_KV_HEADS, HEAD_DIM)
    vp = jnp.pad(v, pad).reshape(B, nb + 2, BLOCK, N_KV_HEADS, HEAD_DIM)
    kb = jnp.concatenate([kp[:, :-2], kp[:, 1:-1], kp[:, 2:]], axis=2)
    vb = jnp.concatenate([vp[:, :-2], vp[:, 1:-1], vp[:, 2:]], axis=2)
    scores = jnp.einsum("bnqhgd,bnkhd->bhgnqk", qb, kb,
                        preferred_element_type=jnp.float32) * (HEAD_DIM ** -0.5)
    qi = jnp.arange(BLOCK)[:, None]
    kj = jnp.arange(3 * BLOCK)[None, :]
    rel = kj - BLOCK - qi
    bias = rel_bias[t5_bucket(rel)].astype(jnp.float32)
    bias = jnp.transpose(bias, (2, 0, 1)).reshape(N_KV_HEADS, G, 1, BLOCK, 3 * BLOCK)
    kabs = jnp.arange(nb)[:, None, None] * BLOCK + kj[None] - BLOCK
    valid = (jnp.abs(rel)[None] <= WINDOW) & (kabs >= 0) & (kabs < S)
    scores = jnp.where(valid, scores + bias, NEG_INF)
    sink_l = sink.astype(jnp.float32).reshape(N_KV_HEADS, G, 1, 1, 1)
    m = jnp.maximum(jnp.max(scores, axis=-1, keepdims=True), sink_l)
    p = jnp.exp(scores - m)
    denom = jnp.sum(p, axis=-1, keepdims=True) + jnp.exp(sink_l - m)
    p = (p / denom).astype(v.dtype)
    out = jnp.einsum("bhgnqk,bnkhd->bnqhgd", p, vb)
    return out.reshape(B, S, N_HEADS * HEAD_DIM)


def hybrid_layer(x, w_in, conv_w, w_a_out, w_pool, pool_scale, w_attn_out, sink, w_o,
                 g_mix, g_ffn, w_gu, w_down, rel_bias):
    h = rms_norm(x, g_mix)
    proj = h @ w_in
    b_a, c_a, x_a, u_p, q, k, v, ga, gp, gt = jnp.split(proj, split_points(), axis=-1)
    y_a = short_conv_mixer(b_a, c_a, x_a, conv_w, w_a_out)
    y_p = multiscale_pool_mixer(u_p, w_pool, pool_scale)
    y_t = windowed_gqa(q, k, v, rel_bias, sink) @ w_attn_out
    merged = jax.nn.sigmoid(ga) * y_a + jax.nn.sigmoid(gp) * y_p + jax.nn.sigmoid(gt) * y_t
    x = x + merged @ w_o
    h2 = rms_norm(x, g_ffn)
    gate, up = jnp.split(h2 @ w_gu, [D_FF], axis=-1)
    return x + (jax.nn.silu(gate) * up) @ w_down


def setup_inputs(seed: int = 0) -> dict:
    key = jax.random.key(seed)
    ks = jax.random.split(key, 16)
    nrm = lambda k, shape, scale: jax.random.normal(k, shape, jnp.float32) * scale
    cg = POOL_WIDTH // POOL_GROUPS
    return {
        "x": nrm(ks[0], (BATCH, SEQ, D_MODEL), 1.0),
        "w_in": nrm(ks[1], (DEPTH, D_MODEL, IN_TOTAL), D_MODEL ** -0.5),
        "conv_w": nrm(ks[2], (DEPTH, CONV_K, 1, CONV_WIDTH), CONV_K ** -0.5),
        "w_a_out": nrm(ks[3], (DEPTH, CONV_WIDTH, D_MODEL), CONV_WIDTH ** -0.5),
        "w_pool": nrm(ks[4], (DEPTH, POOL_GROUPS, cg, cg), cg ** -0.5),
        "pool_scale": 1.0 + nrm(ks[5], (DEPTH, POOL_WIDTH), 0.02),
        "w_attn_out": nrm(ks[6], (DEPTH, Q_WIDTH, D_MODEL), Q_WIDTH ** -0.5),
        "attn_sink": nrm(ks[7], (DEPTH, N_HEADS), 0.5),
        "w_o": nrm(ks[8], (DEPTH, D_MODEL, D_MODEL), D_MODEL ** -0.5),
        "g_mix": 1.0 + nrm(ks[9], (DEPTH, D_MODEL), 0.02),
        "g_ffn": 1.0 + nrm(ks[10], (DEPTH, D_MODEL), 0.02),
        "w_gu": nrm(ks[11], (DEPTH, D_MODEL, 2 * D_FF), D_MODEL ** -0.5),
        "w_down": nrm(ks[12], (DEPTH, D_FF, D_MODEL), D_FF ** -0.5),
        "rel_bias": nrm(ks[13], (N_BUCKETS, N_HEADS), 0.5),
        "g_final": 1.0 + nrm(ks[14], (D_MODEL,), 0.02),
    }


def reference(x, w_in, conv_w, w_a_out, w_pool, pool_scale, w_attn_out, attn_sink, w_o,
              g_mix, g_ffn, w_gu, w_down, rel_bias, g_final):
    for layer in range(DEPTH):
        x = hybrid_layer(x, w_in[layer], conv_w[layer], w_a_out[layer], w_pool[layer],
                         pool_scale[layer], w_attn_out[layer], attn_sink[layer], w_o[layer],
                         g_mix[layer], g_ffn[layer], w_gu[layer], w_down[layer], rel_bias)
    return rms_norm(x, g_final)
```

```python
import functools
import math

import jax
import jax.numpy as jnp
from jax import lax
from jax.experimental import pallas as pl
from jax.experimental.pallas import tpu as pltpu

F32 = jnp.float32
BF16 = jnp.bfloat16

D_MODEL = 1024
N_HEADS = 16
N_KV_HEADS = 4
HEAD_DIM = 64
GROUP = N_HEADS // N_KV_HEADS
WINDOW = 128
BLOCK = 128
N_BUCKETS = 32
MAX_DISTANCE = 128
POOL_GROUPS = 4
POOL_CG = D_MODEL // POOL_GROUPS
D_FF = 2816
EPS = 1e-6
NEG_INF = -1e30

COL_B, COL_C, COL_X, COL_U, COL_Q = 0, 1024, 2048, 3072, 4096
COL_GA, COL_GP, COL_GT = 5120, 6144, 7168
COL_K, COL_V = 8192, 8448
IN_TOTAL = 8704

LANES = 128
HALO = 8
MIB = 1024 * 1024


def _params(vmem_mib, sem):
    return pltpu.CompilerParams(dimension_semantics=sem, vmem_limit_bytes=vmem_mib * MIB)


def _resident(shape, index_map):
    return pl.BlockSpec(shape, index_map, pipeline_mode=pl.Buffered(1))


def _rms(x, g):
    ms = jnp.mean(x * x, axis=-1, keepdims=True)
    return (x * lax.rsqrt(ms + EPS)) * g


def _inproj_kernel(x_ref, g_ref, w_ref, o_ref, h_ref):
    @pl.when(pl.program_id(1) == 0)
    def _():
        h_ref[...] = _rms(x_ref[...], g_ref[...]).astype(BF16)

    o_ref[...] = jnp.dot(h_ref[...], w_ref[...], preferred_element_type=F32).astype(BF16)


def _inproj(x, g, w, *, tm=1024, tn=2176):
    t = x.shape[0]
    return pl.pallas_call(
        _inproj_kernel,
        grid=(t // tm, IN_TOTAL // tn),
        in_specs=[
            pl.BlockSpec((tm, D_MODEL), lambda i, j: (i, 0)),
            pl.BlockSpec((1, D_MODEL), lambda i, j: (0, 0)),
            pl.BlockSpec((D_MODEL, tn), lambda i, j: (0, j)),
        ],
        out_specs=pl.BlockSpec((tm, tn), lambda i, j: (i, j)),
        out_shape=jax.ShapeDtypeStruct((t, IN_TOTAL), BF16),
        scratch_shapes=[pltpu.VMEM((tm, D_MODEL), BF16)],
        compiler_params=_params(48, ("parallel", "arbitrary")),
        name="inproj",
    )(x, g, w)


def _load_ext(ref, c, rows, n_chunks):
    r0 = pl.multiple_of(c * rows, rows)
    main = ref[pl.ds(r0, rows), :].astype(F32)
    p0 = pl.multiple_of(jnp.maximum(r0 - HALO, 0), HALO)
    n0 = pl.multiple_of(jnp.minimum(r0 + rows, (n_chunks - 1) * rows + rows - HALO), HALO)
    prev = ref[pl.ds(p0, HALO), :].astype(F32)
    nxt = ref[pl.ds(n0, HALO), :].astype(F32)
    prev = jnp.where(c == 0, 0.0, prev)
    nxt = jnp.where(c == n_chunks - 1, 0.0, nxt)
    return jnp.concatenate([prev, main, nxt], axis=0)


def _shift(x, k):
    n = x.shape[0]
    return pltpu.roll(x, k % n, 0)


def _conv_kernel(b_ref, c_ref, x_ref, w_ref, o_ref, *, rows):
    seq = b_ref.shape[0]
    n_chunks = seq // rows
    w = w_ref[...]
    w0, w1, w2 = w[0:1, :], w[1:2, :], w[2:3, :]

    def body(c, carry):
        u = _load_ext(c_ref, c, rows, n_chunks) * _load_ext(x_ref, c, rows, n_chunks)
        y = _shift(u, 1) * w0 + u * w1 + _shift(u, -1) * w2
        r0 = pl.multiple_of(c * rows, rows)
        b = b_ref[pl.ds(r0, rows), :].astype(F32)
        o_ref[pl.ds(r0, rows), :] = (b * y[HALO:HALO + rows, :]).astype(BF16)
        return carry

    lax.fori_loop(0, n_chunks, body, 0)


def _conv_mixer(proj, conv_w, *, batch, seq, tk=256, rows=256):
    nk = D_MODEL // tk
    return pl.pallas_call(
        functools.partial(_conv_kernel, rows=rows),
        grid=(batch, nk),
        in_specs=[
            pl.BlockSpec((seq, tk), lambda b, k: (b, COL_B // tk + k)),
            pl.BlockSpec((seq, tk), lambda b, k: (b, COL_C // tk + k)),
            pl.BlockSpec((seq, tk), lambda b, k: (b, COL_X // tk + k)),
            pl.BlockSpec((3, tk), lambda b, k: (0, k)),
        ],
        out_specs=pl.BlockSpec((seq, tk), lambda b, k: (b, k)),
        out_shape=jax.ShapeDtypeStruct((batch * seq, D_MODEL), BF16),
        compiler_params=_params(32, ("parallel", "parallel")),
        name="conv_mixer",
    )(proj, proj, proj, conv_w)


def _pool_kernel(u_ref, w_ref, s_ref, o_ref, *, rows):
    seq = u_ref.shape[0]
    n_chunks = seq // rows
    g = pl.program_id(1)
    win = jnp.left_shift(2, g)
    lo_off = jnp.right_shift(win, 1)
    hi_off = win - 1 - lo_off
    wmat = w_ref[0]
    scale = s_ref[...]

    def body(c, carry):
        u = _load_ext(u_ref, c, rows, n_chunks)
        s2 = u + _shift(u, 1)
        s4 = _shift(s2, -1) + _shift(s2, 1)
        s8 = _shift(s4, 2) + _shift(s4, -2)
        s16 = _shift(s8, 4) + _shift(s8, -4)
        s = jnp.where(g == 0, s2, jnp.where(g == 1, s4, jnp.where(g == 2, s8, s16)))
        s = s[HALO:HALO + rows, :]
        t = c * rows + lax.broadcasted_iota(jnp.int32, (rows, 1), 0)
        cnt = jnp.minimum(t + hi_off, seq - 1) - jnp.maximum(t - lo_off, 0) + 1
        p = s / cnt.astype(F32) - u[HALO:HALO + rows, :]
        y = jnp.dot(p.astype(BF16), wmat, preferred_element_type=F32) * scale
        r0 = pl.multiple_of(c * rows, rows)
        o_ref[pl.ds(r0, rows), :] = y.astype(BF16)
        return carry

    lax.fori_loop(0, n_chunks, body, 0)


def _pool_mixer(proj, w_pool, pool_scale, *, batch, seq, rows=256):
    cg = POOL_CG
    return pl.pallas_call(
        functools.partial(_pool_kernel, rows=rows),
        grid=(batch, POOL_GROUPS),
        in_specs=[
            pl.BlockSpec((seq, cg), lambda b, g: (b, COL_U // cg + g)),
            pl.BlockSpec((1, cg, cg), lambda b, g: (g, 0, 0)),
            pl.BlockSpec((1, cg), lambda b, g: (0, g)),
        ],
        out_specs=pl.BlockSpec((seq, cg), lambda b, g: (b, g)),
        out_shape=jax.ShapeDtypeStruct((batch * seq, D_MODEL), BF16),
        compiler_params=_params(32, ("parallel", "parallel")),
        name="pool_mixer",
    )(proj, w_pool, pool_scale)


def _t5_bucket(rel):
    half = N_BUCKETS // 2
    max_exact = half // 2
    ret = jnp.where(rel > 0, half, 0)
    n = jnp.abs(rel)
    nf = jnp.maximum(n, 1).astype(jnp.float32)
    large = max_exact + (jnp.log(nf / max_exact) / math.log(MAX_DISTANCE / max_exact)
                         * (half - max_exact)).astype(jnp.int32)
    large = jnp.minimum(large, half - 1)
    return ret + jnp.where(n < max_exact, n, large)


def _bucket_map():
    qi = jnp.arange(BLOCK)[:, None]
    kj = jnp.arange(3 * BLOCK)[None, :]
    rel = kj - BLOCK - qi
    return jnp.where(jnp.abs(rel) <= WINDOW, _t5_bucket(rel), -1).astype(jnp.int32)


def _bias_kernel(relb_ref, bk_ref, o_ref):
    h = pl.program_id(0)
    bk = bk_ref[...]
    acc = jnp.full(bk.shape, NEG_INF, F32)
    for b in range(N_BUCKETS):
        acc = jnp.where(bk == b, relb_ref[b, h], acc)
    o_ref[0] = acc


def _bias_table(rel_bias):
    return pl.pallas_call(
        _bias_kernel,
        grid=(N_HEADS,),
        in_specs=[
            pl.BlockSpec(memory_space=pltpu.SMEM),
            pl.BlockSpec((BLOCK, 3 * BLOCK), lambda h: (0, 0)),
        ],
        out_specs=pl.BlockSpec((1, BLOCK, 3 * BLOCK), lambda h: (h, 0, 0)),
        out_shape=jax.ShapeDtypeStruct((N_HEADS, BLOCK, 3 * BLOCK), F32),
        compiler_params=_params(16, ("arbitrary",)),
        name="bias_table",
    )(rel_bias, _bucket_map())


def _attn_kernel(sink_ref, q_ref, kc_ref, kp_ref, kn_ref, vc_ref, vp_ref, vn_ref, bias_ref,
                 o_ref, kd_ref, vd_ref, *, tq, blocks_per_seq):
    i = pl.program_id(0)
    nqb = tq // BLOCK
    pair_w = 2 * HEAD_DIM
    assert pair_w == LANES

    def lane_half(shape):
        return lax.broadcasted_iota(jnp.int32, shape, 1) >= HEAD_DIM

    for dst, parts in ((kd_ref, (kp_ref, kc_ref, kn_ref)), (vd_ref, (vp_ref, vc_ref, vn_ref))):
        for hk in range(N_KV_HEADS):
            c0 = (hk // 2) * pair_w
            upper = (hk % 2) == 1
            r = 0
            for part in parts:
                n = part.shape[0]
                t = part[:, c0:c0 + pair_w].astype(F32)
                sel = lane_half(t.shape) if upper else ~lane_half(t.shape)
                dup = jnp.where(sel, t, pltpu.roll(t, HEAD_DIM, 1))
                dst[hk, r:r + n, :] = dup.astype(BF16)
                r += n

    def q_block(qb, carry):
        n = lax.rem(i * nqb + qb, blocks_per_seq)
        q0 = pl.multiple_of(qb * BLOCK, BLOCK)
        col = lax.broadcasted_iota(jnp.int32, (BLOCK, 3 * BLOCK), 1)
        lo = jnp.where(n == 0, BLOCK, 0)
        hi = jnp.where(n == blocks_per_seq - 1, 2 * BLOCK, 3 * BLOCK)
        valid = (col >= lo) & (col < hi)
        for hk in range(N_KV_HEADS):
            qs = []
            for g in range(GROUP):
                c0 = hk * GROUP * HEAD_DIM + (g // 2) * pair_w
                t = q_ref[pl.ds(q0, BLOCK), c0:c0 + pair_w].astype(F32)
                sel = lane_half(t.shape) if g % 2 else ~lane_half(t.shape)
                qs.append(jnp.where(sel, t * (HEAD_DIM ** -0.5), 0.0).astype(BF16))
            qs = jnp.concatenate(qs, axis=0)
            kwin = kd_ref[hk, pl.ds(q0, 3 * BLOCK), :]
            vwin = vd_ref[hk, pl.ds(q0, 3 * BLOCK), :]
            s = lax.dot_general(qs, kwin, (((1,), (1,)), ((), ())), preferred_element_type=F32)
            ps, ls = [], []
            for g in range(GROUP):
                h = hk * GROUP + g
                sg = s[g * BLOCK:(g + 1) * BLOCK, :] + bias_ref[h]
                sg = jnp.where(valid, sg, NEG_INF)
                sink = sink_ref[h]
                m = jnp.maximum(jnp.max(sg, axis=-1, keepdims=True), sink)
                p = jnp.exp(sg - m)
                ls.append(jnp.sum(p, axis=-1, keepdims=True) + jnp.exp(sink - m))
                ps.append(p.astype(BF16))
            o = jnp.dot(jnp.concatenate(ps, axis=0), vwin, preferred_element_type=F32)
            for j in range(GROUP // 2):
                oa = o[(2 * j) * BLOCK:(2 * j + 1) * BLOCK, :] / ls[2 * j]
                ob = o[(2 * j + 1) * BLOCK:(2 * j + 2) * BLOCK, :] / ls[2 * j + 1]
                tile = jnp.where(lane_half(oa.shape), ob, oa)
                c0 = hk * GROUP * HEAD_DIM + j * pair_w
                o_ref[pl.ds(q0, BLOCK), c0:c0 + pair_w] = tile.astype(BF16)
        return carry

    lax.fori_loop(0, nqb, q_block, 0)


def _attention(proj, bias, sink, *, seq, tq=512):
    t = proj.shape[0]
    assert seq % tq == 0 and tq % BLOCK == 0
    nb = tq // BLOCK
    last_blk = t // BLOCK - 1
    kvw = N_KV_HEADS * HEAD_DIM
    kcol, vcol = COL_K // kvw, COL_V // kvw
    prev_map = lambda c: (lambda i: (jnp.maximum(i * nb - 1, 0), c))
    next_map = lambda c: (lambda i: (jnp.minimum((i + 1) * nb, last_blk), c))
    return pl.pallas_call(
        functools.partial(_attn_kernel, tq=tq, blocks_per_seq=seq // BLOCK),
        grid=(t // tq,),
        in_specs=[
            pl.BlockSpec(memory_space=pltpu.SMEM),
            pl.BlockSpec((tq, D_MODEL), lambda i: (i, COL_Q // D_MODEL)),
            pl.BlockSpec((tq, kvw), lambda i: (i, kcol)),
            pl.BlockSpec((BLOCK, kvw), prev_map(kcol)),
            pl.BlockSpec((BLOCK, kvw), next_map(kcol)),
            pl.BlockSpec((tq, kvw), lambda i: (i, vcol)),
            pl.BlockSpec((BLOCK, kvw), prev_map(vcol)),
            pl.BlockSpec((BLOCK, kvw), next_map(vcol)),
            _resident((N_HEADS, BLOCK, 3 * BLOCK), lambda i: (0, 0, 0)),
        ],
        out_specs=pl.BlockSpec((tq, D_MODEL), lambda i: (i, 0)),
        out_shape=jax.ShapeDtypeStruct((t, D_MODEL), BF16),
        scratch_shapes=[
            pltpu.VMEM((N_KV_HEADS, tq + 2 * BLOCK, LANES), BF16),
            pltpu.VMEM((N_KV_HEADS, tq + 2 * BLOCK, LANES), BF16),
        ],
        compiler_params=_params(40, ("parallel",)),
        name="window_attn",
    )(sink, proj, proj, proj, proj, proj, proj, proj, bias)


def _merge_kernel(a_ref, yp_ref, t_ref, ga_ref, gp_ref, gt_ref, x_ref, wa_ref, wt_ref, wo_ref,
                  o_ref):
    ya = jnp.dot(a_ref[...], wa_ref[...], preferred_element_type=F32)
    yt = jnp.dot(t_ref[...], wt_ref[...], preferred_element_type=F32)
    merged = (jax.nn.sigmoid(ga_ref[...].astype(F32)) * ya
              + jax.nn.sigmoid(gp_ref[...].astype(F32)) * yp_ref[...].astype(F32)
              + jax.nn.sigmoid(gt_ref[...].astype(F32)) * yt)
    o_ref[...] = x_ref[...] + jnp.dot(merged.astype(BF16), wo_ref[...],
                                      preferred_element_type=F32)


def _merge(a_pre, y_p, attn, proj, x, w_a_out, w_attn_out, w_o, *, tm=512):
    t = x.shape[0]
    row = lambda c: pl.BlockSpec((tm, D_MODEL), lambda i: (i, c))
    wspec = _resident((D_MODEL, D_MODEL), lambda i: (0, 0))
    return pl.pallas_call(
        _merge_kernel,
        grid=(t // tm,),
        in_specs=[row(0), row(0), row(0),
                  row(COL_GA // D_MODEL), row(COL_GP // D_MODEL), row(COL_GT // D_MODEL),
                  row(0), wspec, wspec, wspec],
        out_specs=row(0),
        out_shape=jax.ShapeDtypeStruct((t, D_MODEL), F32),
        compiler_params=_params(48, ("parallel",)),
        name="merge_out",
    )(a_pre, y_p, attn, proj, proj, proj, x, w_a_out, w_attn_out, w_o)


def _ffn_kernel(x_ref, g_ref, wgu_ref, wd_ref, gf_ref, o_ref, *, n_chunks, final):
    x = x_ref[...]
    h = _rms(x, g_ref[...]).astype(BF16)
    ck = D_FF // n_chunks
    acc = x
    for c in range(n_chunks):
        gate = jnp.dot(h, wgu_ref[:, c * ck:(c + 1) * ck], preferred_element_type=F32)
        up = jnp.dot(h, wgu_ref[:, D_FF + c * ck:D_FF + (c + 1) * ck],
                     preferred_element_type=F32)
        act = (jax.nn.silu(gate) * up).astype(BF16)
        acc = acc + jnp.dot(act, wd_ref[c * ck:(c + 1) * ck, :], preferred_element_type=F32)
    o_ref[...] = _rms(acc, gf_ref[...]) if final else acc


def _ffn(x, g_ffn, w_gu, w_down, g_final, *, final, tm=512, n_chunks=2):
    t = x.shape[0]
    assert (D_FF // n_chunks) % LANES == 0
    row = pl.BlockSpec((tm, D_MODEL), lambda i: (i, 0))
    vec = pl.BlockSpec((1, D_MODEL), lambda i: (0, 0))
    return pl.pallas_call(
        functools.partial(_ffn_kernel, n_chunks=n_chunks, final=final),
        grid=(t // tm,),
        in_specs=[row, vec,
                  _resident((D_MODEL, 2 * D_FF), lambda i: (0, 0)),
                  _resident((D_FF, D_MODEL), lambda i: (0, 0)),
                  vec],
        out_specs=row,
        out_shape=jax.ShapeDtypeStruct((t, D_MODEL), F32),
        compiler_params=_params(56, ("parallel",)),
        name="ffn",
    )(x, g_ffn, w_gu, w_down, g_final)


def _permute_in_cols(w):
    kv0, kv1 = 5120, 5632
    return jnp.concatenate([w[..., :kv0], w[..., kv1:], w[..., kv0:kv1]], axis=-1)


@jax.jit
def _trunk(x, w_in, conv_w, w_a_out, w_pool, pool_scale, w_attn_out, attn_sink, w_o,
           g_mix, g_ffn, w_gu, w_down, rel_bias, g_final):
    batch, seq, d = x.shape
    depth = w_in.shape[0]
    xf = x.reshape(batch * seq, d)

    w_in_b = _permute_in_cols(w_in).astype(BF16)
    w_a_out_b = w_a_out.astype(BF16)
    w_pool_b = w_pool.astype(BF16)
    w_attn_out_b = w_attn_out.astype(BF16)
    w_o_b = w_o.astype(BF16)
    w_gu_b = w_gu.astype(BF16)
    w_down_b = w_down.astype(BF16)
    conv_w2 = conv_w.reshape(depth, 3, d)
    bias = _bias_table(rel_bias)
    g_final2 = g_final.reshape(1, d)

    for l in range(depth):
        proj = _inproj(xf, g_mix[l].reshape(1, d), w_in_b[l])
        a_pre = _conv_mixer(proj, conv_w2[l], batch=batch, seq=seq)
        y_p = _pool_mixer(proj, w_pool_b[l], pool_scale[l].reshape(1, d), batch=batch, seq=seq)
        attn = _attention(proj, bias, attn_sink[l], seq=seq)
        x1 = _merge(a_pre, y_p, attn, proj, xf, w_a_out_b[l], w_attn_out_b[l], w_o_b[l])
        xf = _ffn(x1, g_ffn[l].reshape(1, d), w_gu_b[l], w_down_b[l], g_final2,
                  final=(l == depth - 1))
    return xf.reshape(batch, seq, d)


def kernel(x, w_in, conv_w, w_a_out, w_pool, pool_scale, w_attn_out, attn_sink, w_o, g_mix,
           g_ffn, w_gu, w_down, rel_bias, g_final):
    return _trunk(x, w_in, conv_w, w_a_out, w_pool, pool_scale, w_attn_out, attn_sink, w_o,
                  g_mix, g_ffn, w_gu, w_down, rel_bias, g_final)
```

```python
import functools
import math

import jax
import jax.numpy as jnp
from jax import lax
from jax.experimental import pallas as pl
from jax.experimental.pallas import tpu as pltpu

F32 = jnp.float32
BF16 = jnp.bfloat16

D_MODEL = 1024
N_HEADS = 16
N_KV_HEADS = 4
HEAD_DIM = 64
GROUP = N_HEADS // N_KV_HEADS
WINDOW = 128
BLOCK = 128
N_BUCKETS = 32
MAX_DISTANCE = 128
POOL_GROUPS = 4
POOL_CG = D_MODEL // POOL_GROUPS
D_FF = 2816
EPS = 1e-6
NEG_INF = -1e30

COL_B, COL_C, COL_X, COL_U, COL_Q = 0, 1024, 2048, 3072, 4096
COL_K, COL_V = 5120, 5376
COL_GA, COL_GP, COL_GT = 5632, 6656, 7680
IN_TOTAL = 8704
GATE_W = 512

LANES = 128
HALO = 8
POOL_HALO = 64
MIB = 1024 * 1024

VAR_MID, VAR_FIRST, VAR_LAST = 0, 1, 2


def _params(vmem_mib, sem):
    return pltpu.CompilerParams(dimension_semantics=sem, vmem_limit_bytes=vmem_mib * MIB)


def _resident(shape, index_map):
    return pl.BlockSpec(shape, index_map, pipeline_mode=pl.Buffered(1))


def _rms(x, g):
    ms = jnp.mean(x * x, axis=-1, keepdims=True)
    return (x * lax.rsqrt(ms + EPS)) * g


def _inproj_kernel(x_ref, g_ref, w_ref, o_ref, h_ref):
    @pl.when(pl.program_id(1) == 0)
    def _():
        h_ref[...] = _rms(x_ref[...], g_ref[...]).astype(BF16)

    o_ref[...] = jnp.dot(h_ref[...], w_ref[...], preferred_element_type=F32).astype(BF16)


def _inproj(x, g, w, layer, *, tm=1024, tn=2176):
    t = x.shape[0]
    return pl.pallas_call(
        _inproj_kernel,
        grid=(t // tm, IN_TOTAL // tn),
        in_specs=[
            pl.BlockSpec((tm, D_MODEL), lambda i, j: (i, 0)),
            pl.BlockSpec((None, 1, D_MODEL), lambda i, j: (layer, 0, 0)),
            pl.BlockSpec((None, D_MODEL, tn), lambda i, j: (layer, 0, j)),
        ],
        out_specs=pl.BlockSpec((tm, tn), lambda i, j: (i, j)),
        out_shape=jax.ShapeDtypeStruct((t, IN_TOTAL), BF16),
        scratch_shapes=[pltpu.VMEM((tm, D_MODEL), BF16)],
        compiler_params=_params(48, ("parallel", "arbitrary")),
        name="inproj",
    )(x, g, w)


def _load_ext(ref, c, rows, n_chunks):
    r0 = pl.multiple_of(c * rows, rows)
    main = ref[pl.ds(r0, rows), :].astype(F32)
    p0 = pl.multiple_of(jnp.maximum(r0 - HALO, 0), HALO)
    n0 = pl.multiple_of(jnp.minimum(r0 + rows, n_chunks * rows - HALO), HALO)
    prev = ref[pl.ds(p0, HALO), :].astype(F32)
    nxt = ref[pl.ds(n0, HALO), :].astype(F32)
    prev = jnp.where(c == 0, 0.0, prev)
    nxt = jnp.where(c == n_chunks - 1, 0.0, nxt)
    return jnp.concatenate([prev, main, nxt], axis=0)


def _shift(x, k):
    n = x.shape[0]
    return pltpu.roll(x, k % n, 0)


def _conv_kernel(b_ref, c_ref, x_ref, w_ref, o_ref, *, rows):
    seq = b_ref.shape[0]
    n_chunks = seq // rows
    w = w_ref[...]
    w0, w1, w2 = w[0:1, :], w[1:2, :], w[2:3, :]

    def body(c, carry):
        u = _load_ext(c_ref, c, rows, n_chunks) * _load_ext(x_ref, c, rows, n_chunks)
        y = _shift(u, 1) * w0 + u * w1 + _shift(u, -1) * w2
        r0 = pl.multiple_of(c * rows, rows)
        b = b_ref[pl.ds(r0, rows), :].astype(F32)
        o_ref[pl.ds(r0, rows), :] = (b * y[HALO:HALO + rows, :]).astype(BF16)
        return carry

    lax.fori_loop(0, n_chunks, body, 0)


def _conv_mixer(proj, conv_w, layer, *, batch, seq, tk=256, rows=256):
    nk = D_MODEL // tk
    return pl.pallas_call(
        functools.partial(_conv_kernel, rows=rows),
        grid=(batch, nk),
        in_specs=[
            pl.BlockSpec((seq, tk), lambda b, k: (b, COL_B // tk + k)),
            pl.BlockSpec((seq, tk), lambda b, k: (b, COL_C // tk + k)),
            pl.BlockSpec((seq, tk), lambda b, k: (b, COL_X // tk + k)),
            pl.BlockSpec((None, 3, tk), lambda b, k: (layer, 0, k)),
        ],
        out_specs=pl.BlockSpec((seq, tk), lambda b, k: (b, k)),
        out_shape=jax.ShapeDtypeStruct((batch * seq, D_MODEL), BF16),
        compiler_params=_params(32, ("parallel", "parallel")),
        name="conv_mixer",
    )(proj, proj, proj, conv_w)


def _pool_kernel(u_ref, w_ref, s_ref, o_ref, band_ref, *, rows, group):
    seq = u_ref.shape[0]
    n_chunks = seq // rows
    assert n_chunks >= 2
    kdim = rows + 2 * POOL_HALO
    g = pl.program_id(1)
    win = jnp.left_shift(2, g)
    lo_off = jnp.right_shift(win, 1)
    hi_off = win - 1 - lo_off

    t_i = lax.broadcasted_iota(jnp.int32, (rows, kdim), 0)
    j_i = lax.broadcasted_iota(jnp.int32, (rows, kdim), 1)
    d = j_i - POOL_HALO - t_i
    inside = (d >= -lo_off) & (d <= hi_off)
    band_ref[VAR_MID] = jnp.where(inside, 1.0, 0.0).astype(BF16)
    band_ref[VAR_FIRST] = jnp.where(inside & (j_i >= POOL_HALO), 1.0, 0.0).astype(BF16)
    band_ref[VAR_LAST] = jnp.where(inside & (j_i < POOL_HALO + rows), 1.0, 0.0).astype(BF16)

    wmat = w_ref[...]
    scale = s_ref[...]

    def window_sums(c):
        r0 = pl.multiple_of(c * rows, rows)
        p0 = pl.multiple_of(jnp.maximum(r0 - POOL_HALO, 0), POOL_HALO)
        n0 = pl.multiple_of(jnp.minimum(r0 + rows, seq - POOL_HALO), POOL_HALO)
        main = u_ref[pl.ds(r0, rows), :]
        ctx = jnp.concatenate(
            [u_ref[pl.ds(p0, POOL_HALO), :], main, u_ref[pl.ds(n0, POOL_HALO), :]], axis=0)
        var = jnp.where(c == 0, VAR_FIRST, jnp.where(c == n_chunks - 1, VAR_LAST, VAR_MID))
        return r0, main, jnp.dot(band_ref[var], ctx, preferred_element_type=F32)

    def centred(r0, main, s):
        t = r0 + lax.broadcasted_iota(jnp.int32, (rows, 1), 0)
        cnt = jnp.minimum(t + hi_off, seq - 1) - jnp.maximum(t - lo_off, 0) + 1
        return (s / cnt.astype(F32) - main.astype(F32)).astype(BF16)

    def body(i, carry):
        sums = [window_sums(i * group + k) for k in range(group)]
        ps = [centred(*args) for args in sums]
        ys = [jnp.dot(p, wmat, preferred_element_type=F32) * scale for p in ps]
        for (r0, _, _), y in zip(sums, ys):
            o_ref[pl.ds(r0, rows), :] = y.astype(BF16)
        return carry

    lax.fori_loop(0, n_chunks // group, body, 0)


def _pool_mixer(proj, w_pool, pool_scale, layer, *, batch, seq, rows=256, group=4):
    cg = POOL_CG
    assert seq % (rows * group) == 0
    return pl.pallas_call(
        functools.partial(_pool_kernel, rows=rows, group=group),
        grid=(batch, POOL_GROUPS),
        in_specs=[
            pl.BlockSpec((seq, cg), lambda b, g: (b, COL_U // cg + g)),
            pl.BlockSpec((None, None, cg, cg), lambda b, g: (layer, g, 0, 0)),
            pl.BlockSpec((None, 1, cg), lambda b, g: (layer, 0, g)),
        ],
        out_specs=pl.BlockSpec((seq, cg), lambda b, g: (b, g)),
        out_shape=jax.ShapeDtypeStruct((batch * seq, D_MODEL), BF16),
        scratch_shapes=[pltpu.VMEM((3, rows, rows + 2 * POOL_HALO), BF16)],
        compiler_params=_params(32, ("parallel", "arbitrary")),
        name="pool_mixer",
    )(proj, w_pool, pool_scale)


def _t5_bucket(rel):
    half = N_BUCKETS // 2
    max_exact = half // 2
    ret = jnp.where(rel > 0, half, 0)
    n = jnp.abs(rel)
    nf = jnp.maximum(n, 1).astype(jnp.float32)
    large = max_exact + (jnp.log(nf / max_exact) / math.log(MAX_DISTANCE / max_exact)
                         * (half - max_exact)).astype(jnp.int32)
    large = jnp.minimum(large, half - 1)
    return ret + jnp.where(n < max_exact, n, large)


def _bucket_map():
    qi = jnp.arange(BLOCK)[:, None]
    kj = jnp.arange(3 * BLOCK)[None, :]
    rel = kj - BLOCK - qi
    return jnp.where(jnp.abs(rel) <= WINDOW, _t5_bucket(rel), -1).astype(jnp.int32)


def _bias_kernel(relb_ref, bk_ref, o_ref):
    var = pl.program_id(0)
    h = pl.program_id(1)
    bk = bk_ref[...]
    acc = jnp.full(bk.shape, NEG_INF, F32)
    for b in range(N_BUCKETS):
        acc = jnp.where(bk == b, relb_ref[b, h], acc)
    col = lax.broadcasted_iota(jnp.int32, bk.shape, 1)
    outside = ((var == VAR_FIRST) & (col < BLOCK)) | ((var == VAR_LAST) & (col >= 2 * BLOCK))
    o_ref[...] = jnp.where(outside, NEG_INF, acc)


def _bias_table(rel_bias):
    return pl.pallas_call(
        _bias_kernel,
        grid=(3, N_HEADS),
        in_specs=[
            pl.BlockSpec(memory_space=pltpu.SMEM),
            pl.BlockSpec((BLOCK, 3 * BLOCK), lambda v, h: (0, 0)),
        ],
        out_specs=pl.BlockSpec((None, None, BLOCK, 3 * BLOCK), lambda v, h: (v, h, 0, 0)),
        out_shape=jax.ShapeDtypeStruct((3, N_HEADS, BLOCK, 3 * BLOCK), F32),
        compiler_params=_params(16, ("arbitrary", "arbitrary")),
        name="bias_table",
    )(rel_bias, _bucket_map())


def _attn_kernel(sink_ref, q_ref, kc_ref, kp_ref, kn_ref, vc_ref, vp_ref, vn_ref, bias_ref,
                 o_ref, klo_ref, khi_ref, vlo_ref, vhi_ref, *, layer, tq, blocks_per_seq):
    i = pl.program_id(0)
    nqb = tq // BLOCK
    assert 2 * HEAD_DIM == LANES and blocks_per_seq >= 2

    def low_half(shape):
        return lax.broadcasted_iota(jnp.int32, shape, 1) < HEAD_DIM

    for lo_ref, hi_ref, parts, scale in ((klo_ref, khi_ref, (kp_ref, kc_ref, kn_ref),
                                          HEAD_DIM ** -0.5),
                                         (vlo_ref, vhi_ref, (vp_ref, vc_ref, vn_ref), None)):
        for a in range(N_KV_HEADS):
            c0 = (a // 2) * LANES
            r = 0
            for part in parts:
                n = part.shape[0]
                t = part[:, c0:c0 + LANES].astype(F32)
                if scale is not None:
                    t = t * scale
                moved = pltpu.roll(t, HEAD_DIM, 1)
                low = low_half(t.shape)
                if a % 2:
                    lo, hi = jnp.where(low, moved, 0.0), jnp.where(low, 0.0, t)
                else:
                    lo, hi = jnp.where(low, t, 0.0), jnp.where(low, 0.0, moved)
                lo_ref[a, r:r + n, :] = lo.astype(BF16)
                hi_ref[a, r:r + n, :] = hi.astype(BF16)
                r += n

    nt_dims = (((1,), (1,)), ((), ()))

    def q_block(qb, carry):
        n = lax.rem(i * nqb + qb, blocks_per_seq)
        var = jnp.where(n == 0, VAR_FIRST, jnp.where(n == blocks_per_seq - 1, VAR_LAST, VAR_MID))
        q0 = pl.multiple_of(qb * BLOCK, BLOCK)
        win = pl.ds(q0, 3 * BLOCK)

        def scores(a):
            c0 = a * GROUP * HEAD_DIM
            qt = jnp.concatenate([q_ref[pl.ds(q0, BLOCK), c0:c0 + LANES],
                                  q_ref[pl.ds(q0, BLOCK), c0 + LANES:c0 + 2 * LANES]], axis=0)
            return (lax.dot_general(qt, klo_ref[a, win, :], nt_dims, preferred_element_type=F32),
                    lax.dot_general(qt, khi_ref[a, win, :], nt_dims, preferred_element_type=F32))

        def softmax(a, s_par):
            ps = [[None, None], [None, None]]
            ls = [[None, None], [None, None]]
            for par in range(2):
                for tile in range(2):
                    h = a * GROUP + 2 * tile + par
                    sg = s_par[par][tile * BLOCK:(tile + 1) * BLOCK, :] + bias_ref[var, h]
                    sink = sink_ref[layer, h]
                    m = jnp.maximum(jnp.max(sg, axis=-1, keepdims=True), sink)
                    p = jnp.exp(sg - m)
                    ls[par][tile] = jnp.sum(p, axis=-1, keepdims=True) + jnp.exp(sink - m)
                    ps[par][tile] = p.astype(BF16)
            return ps, ls

        def values(a, ps, ls):
            c0 = a * GROUP * HEAD_DIM
            o = (jnp.dot(jnp.concatenate(ps[0], axis=0), vlo_ref[a, win, :],
                         preferred_element_type=F32)
                 + jnp.dot(jnp.concatenate(ps[1], axis=0), vhi_ref[a, win, :],
                           preferred_element_type=F32))
            for tile in range(2):
                ot = o[tile * BLOCK:(tile + 1) * BLOCK, :]
                denom = jnp.where(low_half(ot.shape), ls[0][tile], ls[1][tile])
                o_ref[pl.ds(q0, BLOCK), c0 + tile * LANES:c0 + (tile + 1) * LANES] = (
                    ot / denom).astype(BF16)

        s_next = scores(0)
        sm_prev = None
        for a in range(N_KV_HEADS):
            s_cur = s_next
            if a + 1 < N_KV_HEADS:
                s_next = scores(a + 1)
            sm_cur = softmax(a, s_cur)
            if sm_prev is not None:
                values(a - 1, *sm_prev)
            sm_prev = sm_cur
        values(N_KV_HEADS - 1, *sm_prev)
        return carry

    lax.fori_loop(0, nqb, q_block, 0)


def _attention(proj, bias, sink, layer, *, seq, tq=512):
    t = proj.shape[0]
    assert seq % tq == 0 and tq % BLOCK == 0
    nb = tq // BLOCK
    last_blk = t // BLOCK - 1
    kvw = N_KV_HEADS * HEAD_DIM
    kcol, vcol = COL_K // kvw, COL_V // kvw
    prev_map = lambda c: (lambda i: (jnp.maximum(i * nb - 1, 0), c))
    next_map = lambda c: (lambda i: (jnp.minimum((i + 1) * nb, last_blk), c))
    kv_scratch = pltpu.VMEM((N_KV_HEADS, tq + 2 * BLOCK, LANES), BF16)
    return pl.pallas_call(
        functools.partial(_attn_kernel, layer=layer, tq=tq, blocks_per_seq=seq // BLOCK),
        grid=(t // tq,),
        in_specs=[
            pl.BlockSpec(memory_space=pltpu.SMEM),
            pl.BlockSpec((tq, D_MODEL), lambda i: (i, COL_Q // D_MODEL)),
            pl.BlockSpec((tq, kvw), lambda i: (i, kcol)),
            pl.BlockSpec((BLOCK, kvw), prev_map(kcol)),
            pl.BlockSpec((BLOCK, kvw), next_map(kcol)),
            pl.BlockSpec((tq, kvw), lambda i: (i, vcol)),
            pl.BlockSpec((BLOCK, kvw), prev_map(vcol)),
            pl.BlockSpec((BLOCK, kvw), next_map(vcol)),
            _resident((3, N_HEADS, BLOCK, 3 * BLOCK), lambda i: (0, 0, 0, 0)),
        ],
        out_specs=pl.BlockSpec((tq, D_MODEL), lambda i: (i, 0)),
        out_shape=jax.ShapeDtypeStruct((t, D_MODEL), BF16),
        scratch_shapes=[kv_scratch] * 4,
        compiler_params=_params(48, ("parallel",)),
        name="window_attn",
    )(sink, proj, proj, proj, proj, proj, proj, proj, bias)


def _merge_kernel(a_ref, yp_ref, t_ref, ga0_ref, ga1_ref, gp0_ref, gp1_ref, gt0_ref, gt1_ref,
                  x_ref, wa_ref, wt_ref, wo_ref, o_ref):
    def gate(lo_ref, hi_ref):
        g = jnp.concatenate([lo_ref[...], hi_ref[...]], axis=1)
        return jax.nn.sigmoid(g.astype(F32))

    ya = jnp.dot(a_ref[...], wa_ref[...], preferred_element_type=F32)
    yt = jnp.dot(t_ref[...], wt_ref[...], preferred_element_type=F32)
    merged = (gate(ga0_ref, ga1_ref) * ya
              + gate(gp0_ref, gp1_ref) * yp_ref[...].astype(F32)
              + gate(gt0_ref, gt1_ref) * yt)
    o_ref[...] = x_ref[...] + jnp.dot(merged.astype(BF16), wo_ref[...],
                                      preferred_element_type=F32)


def _merge(a_pre, y_p, attn, proj, x, w_a_out, w_attn_out, w_o, layer, *, tm=512):
    t = x.shape[0]
    row = pl.BlockSpec((tm, D_MODEL), lambda i: (i, 0))
    def gate(col):
        def spec(col_block):
            return pl.BlockSpec((tm, GATE_W), lambda i: (i, col_block))
        return [spec(col // GATE_W + half) for half in range(D_MODEL // GATE_W)]
    wspec = _resident((None, D_MODEL, D_MODEL), lambda i: (layer, 0, 0))
    return pl.pallas_call(
        _merge_kernel,
        grid=(t // tm,),
        in_specs=[row, row, row, *gate(COL_GA), *gate(COL_GP), *gate(COL_GT),
                  row, wspec, wspec, wspec],
        out_specs=row,
        out_shape=jax.ShapeDtypeStruct((t, D_MODEL), F32),
        compiler_params=_params(48, ("parallel",)),
        name="merge_out",
    )(a_pre, y_p, attn, proj, proj, proj, proj, proj, proj, x, w_a_out, w_attn_out, w_o)


def _ffn_kernel(x_ref, g_ref, wgu_ref, wd_ref, gf_ref, o_ref, *, n_chunks, final):
    x = x_ref[...]
    h = _rms(x, g_ref[...]).astype(BF16)
    ck = D_FF // n_chunks
    acc = x
    for c in range(n_chunks):
        gate = jnp.dot(h, wgu_ref[:, c * ck:(c + 1) * ck], preferred_element_type=F32)
        up = jnp.dot(h, wgu_ref[:, D_FF + c * ck:D_FF + (c + 1) * ck],
                     preferred_element_type=F32)
        act = (jax.nn.silu(gate) * up).astype(BF16)
        acc = acc + jnp.dot(act, wd_ref[c * ck:(c + 1) * ck, :], preferred_element_type=F32)
    o_ref[...] = _rms(acc, gf_ref[...]) if final else acc


def _ffn(x, g_ffn, w_gu, w_down, g_final, layer, *, final, tm=512, n_chunks=2):
    t = x.shape[0]
    assert (D_FF // n_chunks) % LANES == 0
    row = pl.BlockSpec((tm, D_MODEL), lambda i: (i, 0))
    return pl.pallas_call(
        functools.partial(_ffn_kernel, n_chunks=n_chunks, final=final),
        grid=(t // tm,),
        in_specs=[row,
                  pl.BlockSpec((None, 1, D_MODEL), lambda i: (layer, 0, 0)),
                  _resident((None, D_MODEL, 2 * D_FF), lambda i: (layer, 0, 0)),
                  _resident((None, D_FF, D_MODEL), lambda i: (layer, 0, 0)),
                  pl.BlockSpec((1, D_MODEL), lambda i: (0, 0))],
        out_specs=row,
        out_shape=jax.ShapeDtypeStruct((t, D_MODEL), F32),
        compiler_params=_params(56, ("parallel",)),
        name="ffn",
    )(x, g_ffn, w_gu, w_down, g_final)


@jax.jit
def _trunk(x, w_in, conv_w, w_a_out, w_pool, pool_scale, w_attn_out, attn_sink, w_o,
           g_mix, g_ffn, w_gu, w_down, rel_bias, g_final):
    batch, seq, d = x.shape
    depth = w_in.shape[0]
    xf = x.reshape(batch * seq, d)

    w_in_b = w_in.astype(BF16)
    w_a_out_b = w_a_out.astype(BF16)
    w_pool_b = w_pool.astype(BF16)
    w_attn_out_b = w_attn_out.astype(BF16)
    w_o_b = w_o.astype(BF16)
    w_gu_b = w_gu.astype(BF16)
    w_down_b = w_down.astype(BF16)
    conv_w3 = conv_w.reshape(depth, 3, d)
    pool_scale3 = pool_scale.reshape(depth, 1, d)
    g_mix3 = g_mix.reshape(depth, 1, d)
    g_ffn3 = g_ffn.reshape(depth, 1, d)
    g_final2 = g_final.reshape(1, d)
    bias = _bias_table(rel_bias)

    for l in range(depth):
        proj = _inproj(xf, g_mix3, w_in_b, l)
        a_pre = _conv_mixer(proj, conv_w3, l, batch=batch, seq=seq)
        y_p = _pool_mixer(proj, w_pool_b, pool_scale3, l, batch=batch, seq=seq)
        attn = _attention(proj, bias, attn_sink, l, seq=seq)
        x1 = _merge(a_pre, y_p, attn, proj, xf, w_a_out_b, w_attn_out_b, w_o_b, l)
        xf = _ffn(x1, g_ffn3, w_gu_b, w_down_b, g_final2, l, final=(l == depth - 1))
    return xf.reshape(batch, seq, d)


def kernel(x, w_in, conv_w, w_a_out, w_pool, pool_scale, w_attn_out, attn_sink, w_o, g_mix,
           g_ffn, w_gu, w_down, rel_bias, g_final):
    return _trunk(x, w_in, conv_w, w_a_out, w_pool, pool_scale, w_attn_out, attn_sink, w_o,
                  g_mix, g_ffn, w_gu, w_down, rel_bias, g_final)
```

```python
import functools
import math

import jax
import jax.numpy as jnp
from jax import lax
from jax.experimental import pallas as pl
from jax.experimental.pallas import tpu as pltpu

F32 = jnp.float32
BF16 = jnp.bfloat16

D_MODEL = 1024
N_HEADS = 16
N_KV_HEADS = 4
HEAD_DIM = 64
GROUP = N_HEADS // N_KV_HEADS
WINDOW = 128
BLOCK = 128
N_BUCKETS = 32
MAX_DISTANCE = 128
POOL_GROUPS = 4
POOL_CG = D_MODEL // POOL_GROUPS
D_FF = 2816
EPS = 1e-6
NEG_INF = -1e30

COL_B, COL_C, COL_X, COL_U, COL_Q = 0, 1024, 2048, 3072, 4096
COL_K, COL_V = 5120, 5376
COL_GA, COL_GP, COL_GT = 5632, 6656, 7680
IN_TOTAL = 8704
GATE_W = 512

LANES = 128
MXU_W = 256
HALO = 8
POOL_HALO = 64
MIB = 1024 * 1024

VAR_MID, VAR_FIRST, VAR_LAST = 0, 1, 2


def _params(vmem_mib, sem):
    return pltpu.CompilerParams(dimension_semantics=sem, vmem_limit_bytes=vmem_mib * MIB)


def _resident(shape, index_map):
    return pl.BlockSpec(shape, index_map, pipeline_mode=pl.Buffered(1))


def _rms(x, g):
    ms = jnp.mean(x * x, axis=-1, keepdims=True)
    return (x * lax.rsqrt(ms + EPS)) * g


def _col_chunks(total, width):
    assert total % MXU_W == 0 and width % MXU_W == 0
    return [(c, min(width, total - c)) for c in range(0, total, width)]


def _inproj_kernel(x_ref, g_ref, w_ref, o_ref, *, chunk):
    h = _rms(x_ref[...], g_ref[...]).astype(BF16)
    for c0, n in _col_chunks(IN_TOTAL, chunk):
        o_ref[:, c0:c0 + n] = jnp.dot(h, w_ref[:, c0:c0 + n],
                                      preferred_element_type=F32).astype(BF16)


def _inproj(x, g, w, layer, *, tm=512, chunk=1024):
    t = x.shape[0]
    return pl.pallas_call(
        functools.partial(_inproj_kernel, chunk=chunk),
        grid=(t // tm,),
        in_specs=[
            pl.BlockSpec((tm, D_MODEL), lambda i: (i, 0)),
            pl.BlockSpec((None, 1, D_MODEL), lambda i: (layer, 0, 0)),
            _resident((None, D_MODEL, IN_TOTAL), lambda i: (layer, 0, 0)),
        ],
        out_specs=pl.BlockSpec((tm, IN_TOTAL), lambda i: (i, 0)),
        out_shape=jax.ShapeDtypeStruct((t, IN_TOTAL), BF16),
        compiler_params=_params(48, ("parallel",)),
        name="inproj",
    )(x, g, w)


def _load_ext(ref, c, rows, n_chunks):
    r0 = pl.multiple_of(c * rows, rows)
    main = ref[pl.ds(r0, rows), :].astype(F32)
    p0 = pl.multiple_of(jnp.maximum(r0 - HALO, 0), HALO)
    n0 = pl.multiple_of(jnp.minimum(r0 + rows, n_chunks * rows - HALO), HALO)
    prev = ref[pl.ds(p0, HALO), :].astype(F32)
    nxt = ref[pl.ds(n0, HALO), :].astype(F32)
    prev = jnp.where(c == 0, 0.0, prev)
    nxt = jnp.where(c == n_chunks - 1, 0.0, nxt)
    return jnp.concatenate([prev, main, nxt], axis=0)


def _shift(x, k):
    n = x.shape[0]
    return pltpu.roll(x, k % n, 0)


def _conv_kernel(b_ref, c_ref, x_ref, w_ref, o_ref, *, rows):
    seq = b_ref.shape[0]
    n_chunks = seq // rows
    w = w_ref[...]
    w0, w1, w2 = w[0:1, :], w[1:2, :], w[2:3, :]

    def body(c, carry):
        u = _load_ext(c_ref, c, rows, n_chunks) * _load_ext(x_ref, c, rows, n_chunks)
        y = _shift(u, 1) * w0 + u * w1 + _shift(u, -1) * w2
        r0 = pl.multiple_of(c * rows, rows)
        b = b_ref[pl.ds(r0, rows), :].astype(F32)
        o_ref[pl.ds(r0, rows), :] = (b * y[HALO:HALO + rows, :]).astype(BF16)
        return carry

    lax.fori_loop(0, n_chunks, body, 0)


def _conv_mixer(proj, conv_w, layer, *, batch, seq, tk=256, rows=256):
    nk = D_MODEL // tk
    return pl.pallas_call(
        functools.partial(_conv_kernel, rows=rows),
        grid=(batch, nk),
        in_specs=[
            pl.BlockSpec((seq, tk), lambda b, k: (b, COL_B // tk + k)),
            pl.BlockSpec((seq, tk), lambda b, k: (b, COL_C // tk + k)),
            pl.BlockSpec((seq, tk), lambda b, k: (b, COL_X // tk + k)),
            pl.BlockSpec((None, 3, tk), lambda b, k: (layer, 0, k)),
        ],
        out_specs=pl.BlockSpec((seq, tk), lambda b, k: (b, k)),
        out_shape=jax.ShapeDtypeStruct((batch * seq, D_MODEL), BF16),
        compiler_params=_params(32, ("parallel", "parallel")),
        name="conv_mixer",
    )(proj, proj, proj, conv_w)


def _pool_kernel(u_ref, w_ref, s_ref, o_ref, band_ref, *, rows, group):
    seq = u_ref.shape[0]
    n_chunks = seq // rows
    assert n_chunks >= 2
    kdim = rows + 2 * POOL_HALO
    g = pl.program_id(1)
    win = jnp.left_shift(2, g)
    lo_off = jnp.right_shift(win, 1)
    hi_off = win - 1 - lo_off

    t_i = lax.broadcasted_iota(jnp.int32, (rows, kdim), 0)
    j_i = lax.broadcasted_iota(jnp.int32, (rows, kdim), 1)
    d = j_i - POOL_HALO - t_i
    inside = (d >= -lo_off) & (d <= hi_off)
    band_ref[VAR_MID] = jnp.where(inside, 1.0, 0.0).astype(BF16)
    band_ref[VAR_FIRST] = jnp.where(inside & (j_i >= POOL_HALO), 1.0, 0.0).astype(BF16)
    band_ref[VAR_LAST] = jnp.where(inside & (j_i < POOL_HALO + rows), 1.0, 0.0).astype(BF16)

    wmat = w_ref[...]
    scale = s_ref[...]

    def window_sums(c):
        r0 = pl.multiple_of(c * rows, rows)
        p0 = pl.multiple_of(jnp.maximum(r0 - POOL_HALO, 0), POOL_HALO)
        n0 = pl.multiple_of(jnp.minimum(r0 + rows, seq - POOL_HALO), POOL_HALO)
        main = u_ref[pl.ds(r0, rows), :]
        ctx = jnp.concatenate(
            [u_ref[pl.ds(p0, POOL_HALO), :], main, u_ref[pl.ds(n0, POOL_HALO), :]], axis=0)
        var = jnp.where(c == 0, VAR_FIRST, jnp.where(c == n_chunks - 1, VAR_LAST, VAR_MID))
        return r0, main, jnp.dot(band_ref[var], ctx, preferred_element_type=F32)

    def centred(r0, main, s):
        t = r0 + lax.broadcasted_iota(jnp.int32, (rows, 1), 0)
        cnt = jnp.minimum(t + hi_off, seq - 1) - jnp.maximum(t - lo_off, 0) + 1
        return (s / cnt.astype(F32) - main.astype(F32)).astype(BF16)

    def body(i, carry):
        sums = [window_sums(i * group + k) for k in range(group)]
        ps = [centred(*args) for args in sums]
        ys = [jnp.dot(p, wmat, preferred_element_type=F32) * scale for p in ps]
        for (r0, _, _), y in zip(sums, ys):
            o_ref[pl.ds(r0, rows), :] = y.astype(BF16)
        return carry

    lax.fori_loop(0, n_chunks // group, body, 0)


def _pool_mixer(proj, w_pool, pool_scale, layer, *, batch, seq, rows=256, group=4):
    cg = POOL_CG
    assert seq % (rows * group) == 0
    return pl.pallas_call(
        functools.partial(_pool_kernel, rows=rows, group=group),
        grid=(batch, POOL_GROUPS),
        in_specs=[
            pl.BlockSpec((seq, cg), lambda b, g: (b, COL_U // cg + g)),
            pl.BlockSpec((None, None, cg, cg), lambda b, g: (layer, g, 0, 0)),
            pl.BlockSpec((None, 1, cg), lambda b, g: (layer, 0, g)),
        ],
        out_specs=pl.BlockSpec((seq, cg), lambda b, g: (b, g)),
        out_shape=jax.ShapeDtypeStruct((batch * seq, D_MODEL), BF16),
        scratch_shapes=[pltpu.VMEM((3, rows, rows + 2 * POOL_HALO), BF16)],
        compiler_params=_params(32, ("parallel", "arbitrary")),
        name="pool_mixer",
    )(proj, w_pool, pool_scale)


def _t5_bucket(rel):
    half = N_BUCKETS // 2
    max_exact = half // 2
    ret = jnp.where(rel > 0, half, 0)
    n = jnp.abs(rel)
    nf = jnp.maximum(n, 1).astype(jnp.float32)
    large = max_exact + (jnp.log(nf / max_exact) / math.log(MAX_DISTANCE / max_exact)
                         * (half - max_exact)).astype(jnp.int32)
    large = jnp.minimum(large, half - 1)
    return ret + jnp.where(n < max_exact, n, large)


def _bucket_map():
    qi = jnp.arange(BLOCK)[:, None]
    kj = jnp.arange(3 * BLOCK)[None, :]
    rel = kj - BLOCK - qi
    return jnp.where(jnp.abs(rel) <= WINDOW, _t5_bucket(rel), -1).astype(jnp.int32)


def _bias_kernel(relb_ref, bk_ref, o_ref):
    var = pl.program_id(0)
    h = pl.program_id(1)
    bk = bk_ref[...]
    acc = jnp.full(bk.shape, NEG_INF, F32)
    for b in range(N_BUCKETS):
        acc = jnp.where(bk == b, relb_ref[b, h], acc)
    col = lax.broadcasted_iota(jnp.int32, bk.shape, 1)
    outside = ((var == VAR_FIRST) & (col < BLOCK)) | ((var == VAR_LAST) & (col >= 2 * BLOCK))
    o_ref[...] = jnp.where(outside, NEG_INF, acc)


def _bias_table(rel_bias):
    return pl.pallas_call(
        _bias_kernel,
        grid=(3, N_HEADS),
        in_specs=[
            pl.BlockSpec(memory_space=pltpu.SMEM),
            pl.BlockSpec((BLOCK, 3 * BLOCK), lambda v, h: (0, 0)),
        ],
        out_specs=pl.BlockSpec((None, None, BLOCK, 3 * BLOCK), lambda v, h: (v, h, 0, 0)),
        out_shape=jax.ShapeDtypeStruct((3, N_HEADS, BLOCK, 3 * BLOCK), F32),
        compiler_params=_params(16, ("arbitrary", "arbitrary")),
        name="bias_table",
    )(rel_bias, _bucket_map())


def _attn_kernel(sink_ref, q_ref, kc_ref, kp_ref, kn_ref, vc_ref, vp_ref, vn_ref, bias_ref,
                 o_ref, klo_ref, khi_ref, vlo_ref, vhi_ref, s_ref, p_ref, l_ref,
                 *, layer, tq, blocks_per_seq, sub):
    i = pl.program_id(0)
    nqb = tq // BLOCK
    assert 2 * HEAD_DIM == LANES and blocks_per_seq >= 2 and BLOCK % sub == 0

    def low_half(shape):
        return lax.broadcasted_iota(jnp.int32, shape, 1) < HEAD_DIM

    for lo_ref, hi_ref, parts in ((klo_ref, khi_ref, (kp_ref, kc_ref, kn_ref)),
                                  (vlo_ref, vhi_ref, (vp_ref, vc_ref, vn_ref))):
        for pair in range(N_KV_HEADS // 2):
            r = 0
            for part in parts:
                n = part.shape[0]
                t = pltpu.bitcast(part[:, pair * LANES:(pair + 1) * LANES], jnp.uint32)
                moved = pltpu.roll(t, HEAD_DIM, 1)
                low = low_half(t.shape)
                zero = jnp.zeros_like(t)
                for a, lo, hi in ((2 * pair, jnp.where(low, t, zero), jnp.where(low, zero, moved)),
                                  (2 * pair + 1, jnp.where(low, moved, zero),
                                   jnp.where(low, zero, t))):
                    lo_ref[a, r:r + n, :] = pltpu.bitcast(lo, BF16)
                    hi_ref[a, r:r + n, :] = pltpu.bitcast(hi, BF16)
                r += n

    nt_dims = (((1,), (1,)), ((), ()))
    q_scale = jnp.asarray(HEAD_DIM ** -0.5, BF16)

    def scores(qb, a, slot):
        q0 = qb * BLOCK
        c0 = a * GROUP * HEAD_DIM
        qt = jnp.concatenate([q_ref[pl.ds(q0, BLOCK), c0:c0 + LANES],
                              q_ref[pl.ds(q0, BLOCK), c0 + LANES:c0 + 2 * LANES]], axis=0)
        qt = qt * q_scale
        win = pl.ds(q0, 3 * BLOCK)
        s_ref[slot, 0] = lax.dot_general(qt, klo_ref[a, win, :], nt_dims,
                                         preferred_element_type=F32)
        s_ref[slot, 1] = lax.dot_general(qt, khi_ref[a, win, :], nt_dims,
                                         preferred_element_type=F32)

    def softmax(qb, a, slot):
        n = lax.rem(i * nqb + qb, blocks_per_seq)
        var = jnp.where(n == 0, VAR_FIRST, jnp.where(n == blocks_per_seq - 1, VAR_LAST, VAR_MID))
        for par in range(2):
            for tile in range(2):
                h = a * GROUP + 2 * tile + par
                sink = sink_ref[layer, h]
                for k in range(BLOCK // sub):
                    rows = slice(tile * BLOCK + k * sub, tile * BLOCK + (k + 1) * sub)
                    sg = s_ref[slot, par, rows, :] + bias_ref[var, h, k * sub:(k + 1) * sub, :]
                    m = jnp.maximum(jnp.max(sg, axis=-1, keepdims=True), sink)
                    p = jnp.exp(sg - m)
                    l = jnp.sum(p, axis=-1, keepdims=True) + jnp.exp(sink - m)
                    p_ref[slot, par, rows, :] = p.astype(BF16)
                    l_ref[slot, par, rows, :] = jnp.broadcast_to(l, (sub, LANES))

    def values(qb, a, slot):
        q0 = qb * BLOCK
        c0 = a * GROUP * HEAD_DIM
        win = pl.ds(q0, 3 * BLOCK)
        o = (jnp.dot(p_ref[slot, 0], vlo_ref[a, win, :], preferred_element_type=F32)
             + jnp.dot(p_ref[slot, 1], vhi_ref[a, win, :], preferred_element_type=F32))
        for tile in range(2):
            rows = slice(tile * BLOCK, (tile + 1) * BLOCK)
            denom = jnp.where(low_half((BLOCK, LANES)), l_ref[slot, 0, rows, :],
                              l_ref[slot, 1, rows, :])
            o_ref[pl.ds(q0, BLOCK), c0 + tile * LANES:c0 + (tile + 1) * LANES] = (
                o[rows, :] / denom).astype(BF16)

    units = [(qb, a) for qb in range(nqb) for a in range(N_KV_HEADS)]
    scores(*units[0], 0)
    for t in range(len(units) + 1):
        if t + 1 < len(units):
            scores(*units[t + 1], (t + 1) % 2)
        if t < len(units):
            softmax(*units[t], t % 2)
        if t >= 1:
            values(*units[t - 1], (t - 1) % 2)


def _attention(proj, bias, sink, layer, *, seq, tq=512, sub=64):
    t = proj.shape[0]
    assert seq % tq == 0 and tq % BLOCK == 0
    nb = tq // BLOCK
    last_blk = t // BLOCK - 1
    kvw = N_KV_HEADS * HEAD_DIM
    kcol, vcol = COL_K // kvw, COL_V // kvw
    prev_map = lambda c: (lambda i: (jnp.maximum(i * nb - 1, 0), c))
    next_map = lambda c: (lambda i: (jnp.minimum((i + 1) * nb, last_blk), c))
    kv_scratch = pltpu.VMEM((N_KV_HEADS, tq + 2 * BLOCK, LANES), BF16)
    return pl.pallas_call(
        functools.partial(_attn_kernel, layer=layer, tq=tq, blocks_per_seq=seq // BLOCK,
                          sub=sub),
        grid=(t // tq,),
        in_specs=[
            pl.BlockSpec(memory_space=pltpu.SMEM),
            pl.BlockSpec((tq, D_MODEL), lambda i: (i, COL_Q // D_MODEL)),
            pl.BlockSpec((tq, kvw), lambda i: (i, kcol)),
            pl.BlockSpec((BLOCK, kvw), prev_map(kcol)),
            pl.BlockSpec((BLOCK, kvw), next_map(kcol)),
            pl.BlockSpec((tq, kvw), lambda i: (i, vcol)),
            pl.BlockSpec((BLOCK, kvw), prev_map(vcol)),
            pl.BlockSpec((BLOCK, kvw), next_map(vcol)),
            _resident((3, N_HEADS, BLOCK, 3 * BLOCK), lambda i: (0, 0, 0, 0)),
        ],
        out_specs=pl.BlockSpec((tq, D_MODEL), lambda i: (i, 0)),
        out_shape=jax.ShapeDtypeStruct((t, D_MODEL), BF16),
        scratch_shapes=[kv_scratch] * 4 + [
            pltpu.VMEM((2, 2, 2 * BLOCK, 3 * BLOCK), F32),
            pltpu.VMEM((2, 2, 2 * BLOCK, 3 * BLOCK), BF16),
            pltpu.VMEM((2, 2, 2 * BLOCK, LANES), F32),
        ],
        compiler_params=_params(48, ("parallel",)),
        name="window_attn",
    )(sink, proj, proj, proj, proj, proj, proj, proj, bias)


def _merge_kernel(a_ref, yp_ref, t_ref, ga0_ref, ga1_ref, gp0_ref, gp1_ref, gt0_ref, gt1_ref,
                  x_ref, wa_ref, wt_ref, wo_ref, o_ref):
    def gate(lo_ref, hi_ref):
        g = jnp.concatenate([lo_ref[...], hi_ref[...]], axis=1)
        return jax.nn.sigmoid(g.astype(F32))

    ya = jnp.dot(a_ref[...], wa_ref[...], preferred_element_type=F32)
    yt = jnp.dot(t_ref[...], wt_ref[...], preferred_element_type=F32)
    merged = (gate(ga0_ref, ga1_ref) * ya
              + gate(gp0_ref, gp1_ref) * yp_ref[...].astype(F32)
              + gate(gt0_ref, gt1_ref) * yt)
    o_ref[...] = x_ref[...] + jnp.dot(merged.astype(BF16), wo_ref[...],
                                      preferred_element_type=F32)


def _merge(a_pre, y_p, attn, proj, x, w_a_out, w_attn_out, w_o, layer, *, tm=512):
    t = x.shape[0]
    row = pl.BlockSpec((tm, D_MODEL), lambda i: (i, 0))
    def gate(col):
        def spec(col_block):
            return pl.BlockSpec((tm, GATE_W), lambda i: (i, col_block))
        return [spec(col // GATE_W + half) for half in range(D_MODEL // GATE_W)]
    wspec = _resident((None, D_MODEL, D_MODEL), lambda i: (layer, 0, 0))
    return pl.pallas_call(
        _merge_kernel,
        grid=(t // tm,),
        in_specs=[row, row, row, *gate(COL_GA), *gate(COL_GP), *gate(COL_GT),
                  row, wspec, wspec, wspec],
        out_specs=row,
        out_shape=jax.ShapeDtypeStruct((t, D_MODEL), F32),
        compiler_params=_params(48, ("parallel",)),
        name="merge_out",
    )(a_pre, y_p, attn, proj, proj, proj, proj, proj, proj, x, w_a_out, w_attn_out, w_o)


def _ffn_kernel(x_ref, g_ref, wgu_ref, wd_ref, gf_ref, o_ref, *, chunk, final):
    x = x_ref[...]
    h = _rms(x, g_ref[...]).astype(BF16)
    acc = x
    for c0, n in _col_chunks(D_FF, chunk):
        gate = jnp.dot(h, wgu_ref[:, c0:c0 + n], preferred_element_type=F32)
        up = jnp.dot(h, wgu_ref[:, D_FF + c0:D_FF + c0 + n], preferred_element_type=F32)
        act = (jax.nn.silu(gate) * up).astype(BF16)
        acc = acc + jnp.dot(act, wd_ref[c0:c0 + n, :], preferred_element_type=F32)
    o_ref[...] = _rms(acc, gf_ref[...]) if final else acc


def _ffn(x, g_ffn, w_gu, w_down, g_final, layer, *, final, tm=512, chunk=1536):
    t = x.shape[0]
    row = pl.BlockSpec((tm, D_MODEL), lambda i: (i, 0))
    return pl.pallas_call(
        functools.partial(_ffn_kernel, chunk=chunk, final=final),
        grid=(t // tm,),
        in_specs=[row,
                  pl.BlockSpec((None, 1, D_MODEL), lambda i: (layer, 0, 0)),
                  _resident((None, D_MODEL, 2 * D_FF), lambda i: (layer, 0, 0)),
                  _resident((None, D_FF, D_MODEL), lambda i: (layer, 0, 0)),
                  pl.BlockSpec((1, D_MODEL), lambda i: (0, 0))],
        out_specs=row,
        out_shape=jax.ShapeDtypeStruct((t, D_MODEL), F32),
        compiler_params=_params(56, ("parallel",)),
        name="ffn",
    )(x, g_ffn, w_gu, w_down, g_final)


@jax.jit
def _trunk(x, w_in, conv_w, w_a_out, w_pool, pool_scale, w_attn_out, attn_sink, w_o,
           g_mix, g_ffn, w_gu, w_down, rel_bias, g_final):
    batch, seq, d = x.shape
    depth = w_in.shape[0]
    xf = x.reshape(batch * seq, d)

    w_in_b = w_in.astype(BF16)
    w_a_out_b = w_a_out.astype(BF16)
    w_pool_b = w_pool.astype(BF16)
    w_attn_out_b = w_attn_out.astype(BF16)
    w_o_b = w_o.astype(BF16)
    w_gu_b = w_gu.astype(BF16)
    w_down_b = w_down.astype(BF16)
    conv_w3 = conv_w.reshape(depth, 3, d)
    pool_scale3 = pool_scale.reshape(depth, 1, d)
    g_mix3 = g_mix.reshape(depth, 1, d)
    g_ffn3 = g_ffn.reshape(depth, 1, d)
    g_final2 = g_final.reshape(1, d)
    bias = _bias_table(rel_bias)

    for l in range(depth):
        proj = _inproj(xf, g_mix3, w_in_b, l)
        a_pre = _conv_mixer(proj, conv_w3, l, batch=batch, seq=seq)
        y_p = _pool_mixer(proj, w_pool_b, pool_scale3, l, batch=batch, seq=seq)
        attn = _attention(proj, bias, attn_sink, l, seq=seq)
        x1 = _merge(a_pre, y_p, attn, proj, xf, w_a_out_b, w_attn_out_b, w_o_b, l)
        xf = _ffn(x1, g_ffn3, w_gu_b, w_down_b, g_final2, l, final=(l == depth - 1))
    return xf.reshape(batch, seq, d)


def kernel(x, w_in, conv_w, w_a_out, w_pool, pool_scale, w_attn_out, attn_sink, w_o, g_mix,
           g_ffn, w_gu, w_down, rel_bias, g_final):
    return _trunk(x, w_in, conv_w, w_a_out, w_pool, pool_scale, w_attn_out, attn_sink, w_o,
                  g_mix, g_ffn, w_gu, w_down, rel_bias, g_final)
```

```python
import functools
import math

import jax
import jax.numpy as jnp
from jax import lax
from jax.experimental import pallas as pl
from jax.experimental.pallas import tpu as pltpu

F32 = jnp.float32
BF16 = jnp.bfloat16

D_MODEL = 1024
N_HEADS = 16
N_KV_HEADS = 4
HEAD_DIM = 64
GROUP = N_HEADS // N_KV_HEADS
WINDOW = 128
BLOCK = 128
N_BUCKETS = 32
MAX_DISTANCE = 128
POOL_GROUPS = 4
POOL_CG = D_MODEL // POOL_GROUPS
D_FF = 2816
EPS = 1e-6
NEG_INF = -1e30

COL_B, COL_C, COL_X, COL_U, COL_Q = 0, 1024, 2048, 3072, 4096
COL_K, COL_V = 5120, 5376
COL_GA, COL_GP, COL_GT = 5632, 6656, 7680
IN_TOTAL = 8704
GATE_W = 512

LANES = 128
MXU_W = 256
CONV_HALO = 16
POOL_HALO = 64
POOL_WINDOWS = (2, 4, 8, 16)
MIB = 1024 * 1024

VAR_MID, VAR_FIRST, VAR_LAST = 0, 1, 2


def _params(vmem_mib, sem):
    return pltpu.CompilerParams(dimension_semantics=sem, vmem_limit_bytes=vmem_mib * MIB)


def _resident(shape, index_map):
    return pl.BlockSpec(shape, index_map, pipeline_mode=pl.Buffered(1))


def _rms(x, g):
    ms = jnp.mean(x * x, axis=-1, keepdims=True)
    return (x * lax.rsqrt(ms + EPS)) * g


def _col_chunks(total, width):
    assert total % MXU_W == 0 and width % MXU_W == 0
    return [(c, min(width, total - c)) for c in range(0, total, width)]


def _inproj_kernel(x_ref, g_ref, w_ref, o_ref, *, chunk):
    h = _rms(x_ref[...], g_ref[...]).astype(BF16)
    for c0, n in _col_chunks(IN_TOTAL, chunk):
        o_ref[:, c0:c0 + n] = jnp.dot(h, w_ref[:, c0:c0 + n],
                                      preferred_element_type=F32).astype(BF16)


def _inproj(x, g, w, layer, *, tm=512, chunk=1024):
    t = x.shape[0]
    return pl.pallas_call(
        functools.partial(_inproj_kernel, chunk=chunk),
        grid=(t // tm,),
        in_specs=[
            pl.BlockSpec((tm, D_MODEL), lambda i: (i, 0)),
            pl.BlockSpec((None, 1, D_MODEL), lambda i: (layer, 0, 0)),
            _resident((None, D_MODEL, IN_TOTAL), lambda i: (layer, 0, 0)),
        ],
        out_specs=pl.BlockSpec((tm, IN_TOTAL), lambda i: (i, 0)),
        out_shape=jax.ShapeDtypeStruct((t, IN_TOTAL), BF16),
        compiler_params=_params(48, ("parallel",)),
        name="inproj",
    )(x, g, w)


def _t5_bucket(rel):
    half = N_BUCKETS // 2
    max_exact = half // 2
    ret = jnp.where(rel > 0, half, 0)
    n = jnp.abs(rel)
    nf = jnp.maximum(n, 1).astype(jnp.float32)
    large = max_exact + (jnp.log(nf / max_exact) / math.log(MAX_DISTANCE / max_exact)
                         * (half - max_exact)).astype(jnp.int32)
    large = jnp.minimum(large, half - 1)
    return ret + jnp.where(n < max_exact, n, large)


def _bucket_map():
    qi = jnp.arange(BLOCK)[:, None]
    kj = jnp.arange(3 * BLOCK)[None, :]
    rel = kj - BLOCK - qi
    return jnp.where(jnp.abs(rel) <= WINDOW, _t5_bucket(rel), -1).astype(jnp.int32)


def _bias_kernel(relb_ref, bk_ref, o_ref):
    h = pl.program_id(0)
    bk = bk_ref[...]
    acc = jnp.full(bk.shape, NEG_INF, F32)
    for b in range(N_BUCKETS):
        acc = jnp.where(bk == b, relb_ref[b, h], acc)
    col = lax.broadcasted_iota(jnp.int32, bk.shape, 1)
    o_ref[VAR_MID] = acc
    o_ref[VAR_FIRST] = jnp.where(col < BLOCK, NEG_INF, acc)
    o_ref[VAR_LAST] = jnp.where(col >= 2 * BLOCK, NEG_INF, acc)


def _bias_table(rel_bias):
    return pl.pallas_call(
        _bias_kernel,
        grid=(N_HEADS,),
        in_specs=[
            pl.BlockSpec(memory_space=pltpu.SMEM),
            pl.BlockSpec((BLOCK, 3 * BLOCK), lambda h: (0, 0)),
        ],
        out_specs=pl.BlockSpec((3, None, BLOCK, 3 * BLOCK), lambda h: (0, h, 0, 0)),
        out_shape=jax.ShapeDtypeStruct((3, N_HEADS, BLOCK, 3 * BLOCK), F32),
        compiler_params=_params(16, ("arbitrary",)),
        name="bias_table",
    )(rel_bias, _bucket_map())


def _attn_kernel(sink_ref, q_ref, kc_ref, kp_ref, kn_ref, vc_ref, vp_ref, vn_ref, bias_ref,
                 o_ref, klo_ref, khi_ref, vlo_ref, vhi_ref, s_ref, p_ref, l_ref,
                 *, layer, tq, blocks_per_seq, sub):
    i = pl.program_id(0)
    nqb = tq // BLOCK
    assert 2 * HEAD_DIM == LANES and blocks_per_seq >= 2 and BLOCK % sub == 0

    def low_half(shape):
        return lax.broadcasted_iota(jnp.int32, shape, 1) < HEAD_DIM

    for lo_ref, hi_ref, parts in ((klo_ref, khi_ref, (kp_ref, kc_ref, kn_ref)),
                                  (vlo_ref, vhi_ref, (vp_ref, vc_ref, vn_ref))):
        for pair in range(N_KV_HEADS // 2):
            r = 0
            for part in parts:
                n = part.shape[0]
                t = pltpu.bitcast(part[:, pair * LANES:(pair + 1) * LANES], jnp.uint32)
                moved = pltpu.roll(t, HEAD_DIM, 1)
                low = low_half(t.shape)
                zero = jnp.zeros_like(t)
                for a, lo, hi in ((2 * pair, jnp.where(low, t, zero), jnp.where(low, zero, moved)),
                                  (2 * pair + 1, jnp.where(low, moved, zero),
                                   jnp.where(low, zero, t))):
                    lo_ref[a, r:r + n, :] = pltpu.bitcast(lo, BF16)
                    hi_ref[a, r:r + n, :] = pltpu.bitcast(hi, BF16)
                r += n

    nt_dims = (((1,), (1,)), ((), ()))
    q_scale = jnp.asarray(HEAD_DIM ** -0.5, BF16)

    def scores(qb, a, slot):
        q0 = qb * BLOCK
        c0 = a * GROUP * HEAD_DIM
        qt = jnp.concatenate([q_ref[pl.ds(q0, BLOCK), c0:c0 + LANES],
                              q_ref[pl.ds(q0, BLOCK), c0 + LANES:c0 + 2 * LANES]], axis=0)
        qt = qt * q_scale
        win = pl.ds(q0, 3 * BLOCK)
        s_ref[slot, 0] = lax.dot_general(qt, klo_ref[a, win, :], nt_dims,
                                         preferred_element_type=F32)
        s_ref[slot, 1] = lax.dot_general(qt, khi_ref[a, win, :], nt_dims,
                                         preferred_element_type=F32)

    def softmax(qb, a, slot):
        n = lax.rem(i * nqb + qb, blocks_per_seq)
        var = jnp.where(n == 0, VAR_FIRST, jnp.where(n == blocks_per_seq - 1, VAR_LAST, VAR_MID))
        for par in range(2):
            for tile in range(2):
                h = a * GROUP + 2 * tile + par
                sink = sink_ref[layer, h]
                for k in range(BLOCK // sub):
                    rows = slice(tile * BLOCK + k * sub, tile * BLOCK + (k + 1) * sub)
                    sg = s_ref[slot, par, rows, :] + bias_ref[var, h, k * sub:(k + 1) * sub, :]
                    m = jnp.maximum(jnp.max(sg, axis=-1, keepdims=True), sink)
                    p = jnp.exp(sg - m)
                    l = jnp.sum(p, axis=-1, keepdims=True) + jnp.exp(sink - m)
                    p_ref[slot, par, rows, :] = p.astype(BF16)
                    l_ref[slot, par, rows, :] = jnp.broadcast_to(l, (sub, LANES))

    def values(qb, a, slot):
        q0 = qb * BLOCK
        c0 = a * GROUP * HEAD_DIM
        win = pl.ds(q0, 3 * BLOCK)
        o = (jnp.dot(p_ref[slot, 0], vlo_ref[a, win, :], preferred_element_type=F32)
             + jnp.dot(p_ref[slot, 1], vhi_ref[a, win, :], preferred_element_type=F32))
        for tile in range(2):
            rows = slice(tile * BLOCK, (tile + 1) * BLOCK)
            denom = jnp.where(low_half((BLOCK, LANES)), l_ref[slot, 0, rows, :],
                              l_ref[slot, 1, rows, :])
            o_ref[pl.ds(q0, BLOCK), c0 + tile * LANES:c0 + (tile + 1) * LANES] = (
                o[rows, :] / denom).astype(BF16)

    units = [(qb, a) for qb in range(nqb) for a in range(N_KV_HEADS)]
    scores(*units[0], 0)
    for t in range(len(units) + 1):
        if t + 1 < len(units):
            scores(*units[t + 1], (t + 1) % 2)
        if t < len(units):
            softmax(*units[t], t % 2)
        if t >= 1:
            values(*units[t - 1], (t - 1) % 2)


def _attention(proj, bias, sink, layer, *, seq, tq=512, sub=64):
    t = proj.shape[0]
    assert seq % tq == 0 and tq % BLOCK == 0
    nb = tq // BLOCK
    last_blk = t // BLOCK - 1
    kvw = N_KV_HEADS * HEAD_DIM
    kcol, vcol = COL_K // kvw, COL_V // kvw
    prev_map = lambda c: (lambda i: (jnp.maximum(i * nb - 1, 0), c))
    next_map = lambda c: (lambda i: (jnp.minimum((i + 1) * nb, last_blk), c))
    kv_scratch = pltpu.VMEM((N_KV_HEADS, tq + 2 * BLOCK, LANES), BF16)
    return pl.pallas_call(
        functools.partial(_attn_kernel, layer=layer, tq=tq, blocks_per_seq=seq // BLOCK,
                          sub=sub),
        grid=(t // tq,),
        in_specs=[
            pl.BlockSpec(memory_space=pltpu.SMEM),
            pl.BlockSpec((tq, D_MODEL), lambda i: (i, COL_Q // D_MODEL)),
            pl.BlockSpec((tq, kvw), lambda i: (i, kcol)),
            pl.BlockSpec((BLOCK, kvw), prev_map(kcol)),
            pl.BlockSpec((BLOCK, kvw), next_map(kcol)),
            pl.BlockSpec((tq, kvw), lambda i: (i, vcol)),
            pl.BlockSpec((BLOCK, kvw), prev_map(vcol)),
            pl.BlockSpec((BLOCK, kvw), next_map(vcol)),
            _resident((3, N_HEADS, BLOCK, 3 * BLOCK), lambda i: (0, 0, 0, 0)),
        ],
        out_specs=pl.BlockSpec((tq, D_MODEL), lambda i: (i, 0)),
        out_shape=jax.ShapeDtypeStruct((t, D_MODEL), BF16),
        scratch_shapes=[kv_scratch] * 4 + [
            pltpu.VMEM((2, 2, 2 * BLOCK, 3 * BLOCK), F32),
            pltpu.VMEM((2, 2, 2 * BLOCK, 3 * BLOCK), BF16),
            pltpu.VMEM((2, 2, 2 * BLOCK, LANES), F32),
        ],
        compiler_params=_params(48, ("parallel",)),
        name="window_attn",
    )(sink, proj, proj, proj, proj, proj, proj, proj, bias)


def _shift(x, k):
    n = x.shape[0]
    return pltpu.roll(x, k % n, 0)


def _pool_band(tm):
    t_i = jnp.arange(tm)[:, None]
    j_i = jnp.arange(tm + 2 * POOL_HALO)[None, :]
    d = j_i - POOL_HALO - t_i
    bands = [(d >= -(w // 2)) & (d <= w - 1 - w // 2) for w in POOL_WINDOWS]
    return jnp.stack(bands).astype(BF16)


def _mixer_merge_kernel(b_ref, c_ref, cp_ref, cn_ref, xc_ref, xp_ref, xn_ref,
                        u_ref, up_ref, un_ref, t_ref,
                        ga0_ref, ga1_ref, gp0_ref, gp1_ref, gt0_ref, gt1_ref, x_ref,
                        cw_ref, band_ref, wp_ref, ps_ref, wa_ref, wt_ref, wo_ref,
                        o_ref, *, tiles_per_seq):
    tm = b_ref.shape[0]
    seq = tm * tiles_per_seq
    n = lax.rem(pl.program_id(0), tiles_per_seq)
    first, last = n == 0, n == tiles_per_seq - 1
    cg = POOL_CG
    assert D_MODEL // cg == POOL_GROUPS

    def gate(lo_ref, hi_ref):
        g = jnp.concatenate([lo_ref[...], hi_ref[...]], axis=1)
        return jax.nn.sigmoid(g.astype(F32))

    def conv_chunk(k):
        cols = slice(k * cg, (k + 1) * cg)

        def prod(c_r, x_r):
            return c_r[:, cols].astype(F32) * x_r[:, cols].astype(F32)

        u_ext = jnp.concatenate([jnp.where(first, 0.0, prod(cp_ref, xp_ref)),
                                 prod(c_ref, xc_ref),
                                 jnp.where(last, 0.0, prod(cn_ref, xn_ref))], axis=0)
        w = cw_ref[:, cols]
        y = _shift(u_ext, 1) * w[0:1, :] + u_ext * w[1:2, :] + _shift(u_ext, -1) * w[2:3, :]
        h = cp_ref.shape[0]
        return (b_ref[:, cols].astype(F32) * y[h:h + tm, :]).astype(BF16)

    def window_sum(g):
        cols = slice(g * cg, (g + 1) * cg)
        prev = jnp.where(first, 0.0, up_ref[:, cols].astype(F32)).astype(BF16)
        nxt = jnp.where(last, 0.0, un_ref[:, cols].astype(F32)).astype(BF16)
        ctx = jnp.concatenate([prev, u_ref[:, cols], nxt], axis=0)
        return jnp.dot(band_ref[g], ctx, preferred_element_type=F32)

    t_abs = n * tm + lax.broadcasted_iota(jnp.int32, (tm, 1), 0)

    def pooled(g, s):
        cols = slice(g * cg, (g + 1) * cg)
        win = POOL_WINDOWS[g]
        lo_off, hi_off = win // 2, win - 1 - win // 2
        cnt = jnp.minimum(t_abs + hi_off, seq - 1) - jnp.maximum(t_abs - lo_off, 0) + 1
        p = (s / cnt.astype(F32) - u_ref[:, cols].astype(F32)).astype(BF16)
        return jnp.dot(p, wp_ref[g], preferred_element_type=F32) * ps_ref[:, cols]

    yt = gate(gt0_ref, gt1_ref) * jnp.dot(t_ref[...], wt_ref[...], preferred_element_type=F32)
    sums = [window_sum(g) for g in range(POOL_GROUPS)]
    ya, yps = None, []
    for k in range(POOL_GROUPS):
        part = jnp.dot(conv_chunk(k), wa_ref[k * cg:(k + 1) * cg, :],
                       preferred_element_type=F32)
        ya = part if ya is None else ya + part
        yps.append(pooled(k, sums[k]))
    merged = (gate(ga0_ref, ga1_ref) * ya
              + gate(gp0_ref, gp1_ref) * jnp.concatenate(yps, axis=1)
              + yt)
    o_ref[...] = x_ref[...] + jnp.dot(merged.astype(BF16), wo_ref[...],
                                      preferred_element_type=F32)


def _mixer_merge(proj, attn, x, conv_w, w_pool, pool_scale, w_a_out, w_attn_out, w_o, layer,
                 *, seq, tm=512):
    t = x.shape[0]
    assert seq % tm == 0 and tm % POOL_HALO == 0 and tm % CONV_HALO == 0

    def row(col_block):
        return pl.BlockSpec((tm, D_MODEL), lambda i: (i, col_block))

    def halo(rows, col_block):
        per_tile, last_blk = tm // rows, t // rows - 1
        return (pl.BlockSpec((rows, D_MODEL),
                             lambda i: (jnp.maximum(i * per_tile - 1, 0), col_block)),
                pl.BlockSpec((rows, D_MODEL),
                             lambda i: (jnp.minimum((i + 1) * per_tile, last_blk), col_block)))

    def gate(col):
        def spec(col_block):
            return pl.BlockSpec((tm, GATE_W), lambda i: (i, col_block))
        return [spec(col // GATE_W + half) for half in range(D_MODEL // GATE_W)]

    wspec = _resident((None, D_MODEL, D_MODEL), lambda i: (layer, 0, 0))
    band = _pool_band(tm)
    c_blk, x_blk, u_blk = COL_C // D_MODEL, COL_X // D_MODEL, COL_U // D_MODEL
    return pl.pallas_call(
        functools.partial(_mixer_merge_kernel, tiles_per_seq=seq // tm),
        grid=(t // tm,),
        in_specs=[row(COL_B // D_MODEL),
                  row(c_blk), *halo(CONV_HALO, c_blk),
                  row(x_blk), *halo(CONV_HALO, x_blk),
                  row(u_blk), *halo(POOL_HALO, u_blk),
                  row(0),
                  *gate(COL_GA), *gate(COL_GP), *gate(COL_GT),
                  row(0),
                  pl.BlockSpec((None, 3, D_MODEL), lambda i: (layer, 0, 0)),
                  _resident(band.shape, lambda i: (0, 0, 0)),
                  _resident((None, POOL_GROUPS, POOL_CG, POOL_CG), lambda i: (layer, 0, 0, 0)),
                  pl.BlockSpec((None, 1, D_MODEL), lambda i: (layer, 0, 0)),
                  wspec, wspec, wspec],
        out_specs=row(0),
        out_shape=jax.ShapeDtypeStruct((t, D_MODEL), F32),
        compiler_params=_params(56, ("parallel",)),
        name="mixer_merge",
    )(proj, proj, proj, proj, proj, proj, proj, proj, proj, proj, attn,
      proj, proj, proj, proj, proj, proj, x,
      conv_w, band, w_pool, pool_scale, w_a_out, w_attn_out, w_o)


def _ffn_kernel(x_ref, g_ref, wgu_ref, wd_ref, gf_ref, o_ref, *, chunk, final):
    x = x_ref[...]
    h = _rms(x, g_ref[...]).astype(BF16)
    acc = x
    for c0, n in _col_chunks(D_FF, chunk):
        gate = jnp.dot(h, wgu_ref[:, c0:c0 + n], preferred_element_type=F32)
        up = jnp.dot(h, wgu_ref[:, D_FF + c0:D_FF + c0 + n], preferred_element_type=F32)
        act = (jax.nn.silu(gate) * up).astype(BF16)
        acc = acc + jnp.dot(act, wd_ref[c0:c0 + n, :], preferred_element_type=F32)
    o_ref[...] = _rms(acc, gf_ref[...]) if final else acc


def _ffn(x, g_ffn, w_gu, w_down, g_final, layer, *, final, tm=512, chunk=1536):
    t = x.shape[0]
    row = pl.BlockSpec((tm, D_MODEL), lambda i: (i, 0))
    return pl.pallas_call(
        functools.partial(_ffn_kernel, chunk=chunk, final=final),
        grid=(t // tm,),
        in_specs=[row,
                  pl.BlockSpec((None, 1, D_MODEL), lambda i: (layer, 0, 0)),
                  _resident((None, D_MODEL, 2 * D_FF), lambda i: (layer, 0, 0)),
                  _resident((None, D_FF, D_MODEL), lambda i: (layer, 0, 0)),
                  pl.BlockSpec((1, D_MODEL), lambda i: (0, 0))],
        out_specs=row,
        out_shape=jax.ShapeDtypeStruct((t, D_MODEL), F32),
        compiler_params=_params(56, ("parallel",)),
        name="ffn",
    )(x, g_ffn, w_gu, w_down, g_final)


@jax.jit
def _trunk(x, w_in, conv_w, w_a_out, w_pool, pool_scale, w_attn_out, attn_sink, w_o,
           g_mix, g_ffn, w_gu, w_down, rel_bias, g_final):
    batch, seq, d = x.shape
    depth = w_in.shape[0]
    xf = x.reshape(batch * seq, d)

    w_in_b = w_in.astype(BF16)
    w_a_out_b = w_a_out.astype(BF16)
    w_pool_b = w_pool.astype(BF16)
    w_attn_out_b = w_attn_out.astype(BF16)
    w_o_b = w_o.astype(BF16)
    w_gu_b = w_gu.astype(BF16)
    w_down_b = w_down.astype(BF16)
    conv_w3 = conv_w.reshape(depth, 3, d)
    pool_scale3 = pool_scale.reshape(depth, 1, d)
    g_mix3 = g_mix.reshape(depth, 1, d)
    g_ffn3 = g_ffn.reshape(depth, 1, d)
    g_final2 = g_final.reshape(1, d)
    bias = _bias_table(rel_bias)

    for l in range(depth):
        proj = _inproj(xf, g_mix3, w_in_b, l)
        attn = _attention(proj, bias, attn_sink, l, seq=seq)
        x1 = _mixer_merge(proj, attn, xf, conv_w3, w_pool_b, pool_scale3, w_a_out_b,
                          w_attn_out_b, w_o_b, l, seq=seq)
        xf = _ffn(x1, g_ffn3, w_gu_b, w_down_b, g_final2, l, final=(l == depth - 1))
    return xf.reshape(batch, seq, d)


def kernel(x, w_in, conv_w, w_a_out, w_pool, pool_scale, w_attn_out, attn_sink, w_o, g_mix,
           g_ffn, w_gu, w_down, rel_bias, g_final):
    return _trunk(x, w_in, conv_w, w_a_out, w_pool, pool_scale, w_attn_out, attn_sink, w_o,
                  g_mix, g_ffn, w_gu, w_down, rel_bias, g_final)
```

```python
import functools
import math

import jax
import jax.numpy as jnp
from jax import lax
from jax.experimental import pallas as pl
from jax.experimental.pallas import tpu as pltpu

F32 = jnp.float32
BF16 = jnp.bfloat16

D_MODEL = 1024
N_HEADS = 16
N_KV_HEADS = 4
HEAD_DIM = 64
GROUP = N_HEADS // N_KV_HEADS
WINDOW = 128
BLOCK = 128
N_BUCKETS = 32
MAX_DISTANCE = 128
POOL_GROUPS = 4
POOL_CG = D_MODEL // POOL_GROUPS
D_FF = 2816
EPS = 1e-6
NEG_INF = -1e30

COL_B, COL_C, COL_X, COL_U, COL_Q = 0, 1024, 2048, 3072, 4096
COL_K, COL_V = 5120, 5376
COL_GA, COL_GP, COL_GT = 5632, 6656, 7680
IN_TOTAL = 8704
GATE_W = 512

LANES = 128
MXU_W = 256
CONV_HALO = 16
POOL_HALO = 64
POOL_WINDOWS = (2, 4, 8, 16)
MIB = 1024 * 1024

VAR_MID, VAR_FIRST, VAR_LAST = 0, 1, 2


def _params(vmem_mib, sem):
    return pltpu.CompilerParams(dimension_semantics=sem, vmem_limit_bytes=vmem_mib * MIB)


def _resident(shape, index_map):
    return pl.BlockSpec(shape, index_map, pipeline_mode=pl.Buffered(1))


def _rms(x, g):
    ms = jnp.mean(x * x, axis=-1, keepdims=True)
    return (x * lax.rsqrt(ms + EPS)) * g


def _col_chunks(total, width):
    assert total % MXU_W == 0 and width % MXU_W == 0
    return [(c, min(width, total - c)) for c in range(0, total, width)]


def _inproj_kernel(x_ref, g_ref, w_ref, o_ref, *, chunk):
    h = _rms(x_ref[...], g_ref[...]).astype(BF16)
    for c0, n in _col_chunks(IN_TOTAL, chunk):
        o_ref[:, c0:c0 + n] = jnp.dot(h, w_ref[:, c0:c0 + n],
                                      preferred_element_type=F32).astype(BF16)


def _inproj(x, g, w, layer, *, tm=512, chunk=1024):
    t = x.shape[0]
    return pl.pallas_call(
        functools.partial(_inproj_kernel, chunk=chunk),
        grid=(t // tm,),
        in_specs=[
            pl.BlockSpec((tm, D_MODEL), lambda i: (i, 0)),
            pl.BlockSpec((None, 1, D_MODEL), lambda i: (layer, 0, 0)),
            _resident((None, D_MODEL, IN_TOTAL), lambda i: (layer, 0, 0)),
        ],
        out_specs=pl.BlockSpec((tm, IN_TOTAL), lambda i: (i, 0)),
        out_shape=jax.ShapeDtypeStruct((t, IN_TOTAL), BF16),
        compiler_params=_params(48, ("parallel",)),
        name="inproj",
    )(x, g, w)


def _t5_bucket(rel):
    half = N_BUCKETS // 2
    max_exact = half // 2
    ret = jnp.where(rel > 0, half, 0)
    n = jnp.abs(rel)
    nf = jnp.maximum(n, 1).astype(jnp.float32)
    large = max_exact + (jnp.log(nf / max_exact) / math.log(MAX_DISTANCE / max_exact)
                         * (half - max_exact)).astype(jnp.int32)
    large = jnp.minimum(large, half - 1)
    return ret + jnp.where(n < max_exact, n, large)


def _bucket_map():
    kj = jnp.arange(3 * BLOCK)[:, None]
    qi = jnp.arange(BLOCK)[None, :]
    rel = kj - BLOCK - qi
    return jnp.where(jnp.abs(rel) <= WINDOW, _t5_bucket(rel), -1).astype(jnp.int32)


def _bias_kernel(relb_ref, bk_ref, o_ref):
    h = pl.program_id(0)
    bk = bk_ref[...]
    acc = jnp.full(bk.shape, NEG_INF, F32)
    for b in range(N_BUCKETS):
        acc = jnp.where(bk == b, relb_ref[b, h], acc)
    key = lax.broadcasted_iota(jnp.int32, bk.shape, 0)
    o_ref[VAR_MID] = acc
    o_ref[VAR_FIRST] = jnp.where(key < BLOCK, NEG_INF, acc)
    o_ref[VAR_LAST] = jnp.where(key >= 2 * BLOCK, NEG_INF, acc)


def _bias_table(rel_bias):
    return pl.pallas_call(
        _bias_kernel,
        grid=(N_HEADS,),
        in_specs=[
            pl.BlockSpec(memory_space=pltpu.SMEM),
            pl.BlockSpec((3 * BLOCK, BLOCK), lambda h: (0, 0)),
        ],
        out_specs=pl.BlockSpec((3, None, 3 * BLOCK, BLOCK), lambda h: (0, h, 0, 0)),
        out_shape=jax.ShapeDtypeStruct((3, N_HEADS, 3 * BLOCK, BLOCK), F32),
        compiler_params=_params(16, ("arbitrary",)),
        name="bias_table",
    )(rel_bias, _bucket_map())


def _attn_kernel(sink_ref, q_ref, kc_ref, kp_ref, kn_ref, vc_ref, vp_ref, vn_ref, bias_ref,
                 o_ref, klo_ref, khi_ref, vt_ref, *, layer, tq, blocks_per_seq):
    i = pl.program_id(0)
    nqb = tq // BLOCK
    assert 2 * HEAD_DIM == LANES and blocks_per_seq >= 2

    for pair in range(N_KV_HEADS // 2):
        r = 0
        for part in (kp_ref, kc_ref, kn_ref):
            n = part.shape[0]
            t = pltpu.bitcast(part[:, pair * LANES:(pair + 1) * LANES], jnp.uint32)
            moved = pltpu.roll(t, HEAD_DIM, 1)
            low = lax.broadcasted_iota(jnp.int32, t.shape, 1) < HEAD_DIM
            zero = jnp.zeros_like(t)
            for a, lo, hi in ((2 * pair, jnp.where(low, t, zero), jnp.where(low, zero, moved)),
                              (2 * pair + 1, jnp.where(low, moved, zero),
                               jnp.where(low, zero, t))):
                klo_ref[a, r:r + n, :] = pltpu.bitcast(lo, BF16)
                khi_ref[a, r:r + n, :] = pltpu.bitcast(hi, BF16)
            r += n
        c = 0
        for part in (vp_ref, vc_ref, vn_ref):
            for b0 in range(0, part.shape[0], BLOCK):
                blk = part[b0:b0 + BLOCK, pair * LANES:(pair + 1) * LANES].astype(F32)
                vt_ref[pair, :, c:c + BLOCK] = blk.T.astype(BF16)
                c += BLOCK

    nt_dims = (((1,), (1,)), ((), ()))
    q_scale = jnp.asarray(HEAD_DIM ** -0.5, BF16)

    def scores(qb, a):
        q0 = qb * BLOCK
        c0 = a * GROUP * HEAD_DIM
        qt = jnp.concatenate([q_ref[q0:q0 + BLOCK, c0:c0 + LANES],
                              q_ref[q0:q0 + BLOCK, c0 + LANES:c0 + 2 * LANES]], axis=0)
        qt = qt * q_scale
        win = slice(q0, q0 + 3 * BLOCK)
        return [lax.dot_general(k_ref[a, win, :], qt, nt_dims, preferred_element_type=F32)
                for k_ref in (klo_ref, khi_ref)]

    def softmax(qb, a, s_par):
        n = lax.rem(i * nqb + qb, blocks_per_seq)
        var = jnp.where(n == 0, VAR_FIRST, jnp.where(n == blocks_per_seq - 1, VAR_LAST, VAR_MID))
        pts, ls = [], []
        for par in range(2):
            p_t, l_t = [], []
            for tile in range(2):
                h = a * GROUP + 2 * tile + par
                sink = sink_ref[layer, h]
                sg = s_par[par][:, tile * BLOCK:(tile + 1) * BLOCK] + bias_ref[var, h]
                m = jnp.maximum(jnp.max(sg, axis=0, keepdims=True), sink)
                p = jnp.exp(sg - m)
                l_t.append(jnp.sum(p, axis=0, keepdims=True) + jnp.exp(sink - m))
                p_t.append(p.astype(BF16))
            pts.append(jnp.concatenate(p_t, axis=1))
            ls.append(jnp.concatenate(l_t, axis=1))
        return pts, ls

    def values(qb, a, pts, ls):
        q0 = qb * BLOCK
        c0 = a * GROUP * HEAD_DIM
        r0 = (a % 2) * HEAD_DIM
        v_t = vt_ref[a // 2, r0:r0 + HEAD_DIM, q0:q0 + 3 * BLOCK]
        o_t = jnp.concatenate(
            [jnp.dot(v_t, pts[par], preferred_element_type=F32) / ls[par] for par in range(2)],
            axis=0)
        for tile in range(2):
            o_ref[q0:q0 + BLOCK, c0 + tile * LANES:c0 + (tile + 1) * LANES] = (
                o_t[:, tile * BLOCK:(tile + 1) * BLOCK].T.astype(BF16))

    units = [(qb, a) for qb in range(nqb) for a in range(N_KV_HEADS)]
    s_next = scores(*units[0])
    sm_prev = None
    for t in range(len(units) + 1):
        s_cur = s_next
        if t + 1 < len(units):
            s_next = scores(*units[t + 1])
        sm_cur = softmax(*units[t], s_cur) if t < len(units) else None
        if t >= 1:
            values(*units[t - 1], *sm_prev)
        sm_prev = sm_cur


def _attention(proj, bias, sink, layer, *, seq, tq=512):
    t = proj.shape[0]
    assert seq % tq == 0 and tq % BLOCK == 0
    nb = tq // BLOCK
    last_blk = t // BLOCK - 1
    kvw = N_KV_HEADS * HEAD_DIM
    kcol, vcol = COL_K // kvw, COL_V // kvw
    prev_map = lambda c: (lambda i: (jnp.maximum(i * nb - 1, 0), c))
    next_map = lambda c: (lambda i: (jnp.minimum((i + 1) * nb, last_blk), c))
    k_scratch = pltpu.VMEM((N_KV_HEADS, tq + 2 * BLOCK, LANES), BF16)
    vt_scratch = pltpu.VMEM((N_KV_HEADS // 2, LANES, tq + 2 * BLOCK), BF16)
    return pl.pallas_call(
        functools.partial(_attn_kernel, layer=layer, tq=tq, blocks_per_seq=seq // BLOCK),
        grid=(t // tq,),
        in_specs=[
            pl.BlockSpec(memory_space=pltpu.SMEM),
            pl.BlockSpec((tq, D_MODEL), lambda i: (i, COL_Q // D_MODEL)),
            pl.BlockSpec((tq, kvw), lambda i: (i, kcol)),
            pl.BlockSpec((BLOCK, kvw), prev_map(kcol)),
            pl.BlockSpec((BLOCK, kvw), next_map(kcol)),
            pl.BlockSpec((tq, kvw), lambda i: (i, vcol)),
            pl.BlockSpec((BLOCK, kvw), prev_map(vcol)),
            pl.BlockSpec((BLOCK, kvw), next_map(vcol)),
            _resident((3, N_HEADS, 3 * BLOCK, BLOCK), lambda i: (0, 0, 0, 0)),
        ],
        out_specs=pl.BlockSpec((tq, D_MODEL), lambda i: (i, 0)),
        out_shape=jax.ShapeDtypeStruct((t, D_MODEL), BF16),
        scratch_shapes=[k_scratch, k_scratch, vt_scratch],
        compiler_params=_params(48, ("parallel",)),
        name="window_attn",
    )(sink, proj, proj, proj, proj, proj, proj, proj, bias)


def _shift(x, k):
    n = x.shape[0]
    return pltpu.roll(x, k % n, 0)


def _pool_band(tm):
    t_i = jnp.arange(tm)[:, None]
    j_i = jnp.arange(tm + 2 * POOL_HALO)[None, :]
    d = j_i - POOL_HALO - t_i
    bands = [(d >= -(w // 2)) & (d <= w - 1 - w // 2) for w in POOL_WINDOWS]
    return jnp.stack(bands).astype(BF16)


def _mixer_merge_kernel(b_ref, c_ref, cp_ref, cn_ref, xc_ref, xp_ref, xn_ref,
                        u_ref, up_ref, un_ref, t_ref,
                        ga0_ref, ga1_ref, gp0_ref, gp1_ref, gt0_ref, gt1_ref, x_ref,
                        cw_ref, band_ref, wp_ref, ps_ref, wa_ref, wt_ref, wo_ref,
                        o_ref, *, tiles_per_seq):
    tm = b_ref.shape[0]
    seq = tm * tiles_per_seq
    n = lax.rem(pl.program_id(0), tiles_per_seq)
    first, last = n == 0, n == tiles_per_seq - 1
    cg = POOL_CG
    assert D_MODEL // cg == POOL_GROUPS

    def gate(lo_ref, hi_ref):
        g = jnp.concatenate([lo_ref[...], hi_ref[...]], axis=1)
        return jax.nn.sigmoid(g.astype(F32))

    def conv_chunk(k):
        cols = slice(k * cg, (k + 1) * cg)

        def prod(c_r, x_r):
            return c_r[:, cols].astype(F32) * x_r[:, cols].astype(F32)

        u_ext = jnp.concatenate([jnp.where(first, 0.0, prod(cp_ref, xp_ref)),
                                 prod(c_ref, xc_ref),
                                 jnp.where(last, 0.0, prod(cn_ref, xn_ref))], axis=0)
        w = cw_ref[:, cols]
        y = _shift(u_ext, 1) * w[0:1, :] + u_ext * w[1:2, :] + _shift(u_ext, -1) * w[2:3, :]
        h = cp_ref.shape[0]
        return (b_ref[:, cols].astype(F32) * y[h:h + tm, :]).astype(BF16)

    def window_sum(g):
        cols = slice(g * cg, (g + 1) * cg)
        prev = jnp.where(first, 0.0, up_ref[:, cols].astype(F32)).astype(BF16)
        nxt = jnp.where(last, 0.0, un_ref[:, cols].astype(F32)).astype(BF16)
        ctx = jnp.concatenate([prev, u_ref[:, cols], nxt], axis=0)
        return jnp.dot(band_ref[g], ctx, preferred_element_type=F32)

    t_abs = n * tm + lax.broadcasted_iota(jnp.int32, (tm, 1), 0)

    def pooled(g, s):
        cols = slice(g * cg, (g + 1) * cg)
        win = POOL_WINDOWS[g]
        lo_off, hi_off = win // 2, win - 1 - win // 2
        cnt = jnp.minimum(t_abs + hi_off, seq - 1) - jnp.maximum(t_abs - lo_off, 0) + 1
        p = (s / cnt.astype(F32) - u_ref[:, cols].astype(F32)).astype(BF16)
        return jnp.dot(p, wp_ref[g], preferred_element_type=F32) * ps_ref[:, cols]

    yt = gate(gt0_ref, gt1_ref) * jnp.dot(t_ref[...], wt_ref[...], preferred_element_type=F32)
    sums = [window_sum(g) for g in range(POOL_GROUPS)]
    ya, yps = None, []
    for k in range(POOL_GROUPS):
        part = jnp.dot(conv_chunk(k), wa_ref[k * cg:(k + 1) * cg, :],
                       preferred_element_type=F32)
        ya = part if ya is None else ya + part
        yps.append(pooled(k, sums[k]))
    merged = (gate(ga0_ref, ga1_ref) * ya
              + gate(gp0_ref, gp1_ref) * jnp.concatenate(yps, axis=1)
              + yt)
    o_ref[...] = x_ref[...] + jnp.dot(merged.astype(BF16), wo_ref[...],
                                      preferred_element_type=F32)


def _mixer_merge(proj, attn, x, conv_w, w_pool, pool_scale, w_a_out, w_attn_out, w_o, layer,
                 *, seq, tm=512):
    t = x.shape[0]
    assert seq % tm == 0 and tm % POOL_HALO == 0 and tm % CONV_HALO == 0

    def row(col_block):
        return pl.BlockSpec((tm, D_MODEL), lambda i: (i, col_block))

    def halo(rows, col_block):
        per_tile, last_blk = tm // rows, t // rows - 1
        return (pl.BlockSpec((rows, D_MODEL),
                             lambda i: (jnp.maximum(i * per_tile - 1, 0), col_block)),
                pl.BlockSpec((rows, D_MODEL),
                             lambda i: (jnp.minimum((i + 1) * per_tile, last_blk), col_block)))

    def gate(col):
        def spec(col_block):
            return pl.BlockSpec((tm, GATE_W), lambda i: (i, col_block))
        return [spec(col // GATE_W + half) for half in range(D_MODEL // GATE_W)]

    wspec = _resident((None, D_MODEL, D_MODEL), lambda i: (layer, 0, 0))
    band = _pool_band(tm)
    c_blk, x_blk, u_blk = COL_C // D_MODEL, COL_X // D_MODEL, COL_U // D_MODEL
    return pl.pallas_call(
        functools.partial(_mixer_merge_kernel, tiles_per_seq=seq // tm),
        grid=(t // tm,),
        in_specs=[row(COL_B // D_MODEL),
                  row(c_blk), *halo(CONV_HALO, c_blk),
                  row(x_blk), *halo(CONV_HALO, x_blk),
                  row(u_blk), *halo(POOL_HALO, u_blk),
                  row(0),
                  *gate(COL_GA), *gate(COL_GP), *gate(COL_GT),
                  row(0),
                  pl.BlockSpec((None, 3, D_MODEL), lambda i: (layer, 0, 0)),
                  _resident(band.shape, lambda i: (0, 0, 0)),
                  _resident((None, POOL_GROUPS, POOL_CG, POOL_CG), lambda i: (layer, 0, 0, 0)),
                  pl.BlockSpec((None, 1, D_MODEL), lambda i: (layer, 0, 0)),
                  wspec, wspec, wspec],
        out_specs=row(0),
        out_shape=jax.ShapeDtypeStruct((t, D_MODEL), F32),
        compiler_params=_params(56, ("parallel",)),
        name="mixer_merge",
    )(proj, proj, proj, proj, proj, proj, proj, proj, proj, proj, attn,
      proj, proj, proj, proj, proj, proj, x,
      conv_w, band, w_pool, pool_scale, w_a_out, w_attn_out, w_o)


def _ffn_kernel(x_ref, g_ref, wgu_ref, wd_ref, gf_ref, o_ref, *, chunk, final):
    x = x_ref[...]
    h = _rms(x, g_ref[...]).astype(BF16)
    acc = x
    for c0, n in _col_chunks(D_FF, chunk):
        gate = jnp.dot(h, wgu_ref[:, c0:c0 + n], preferred_element_type=F32)
        up = jnp.dot(h, wgu_ref[:, D_FF + c0:D_FF + c0 + n], preferred_element_type=F32)
        act = (jax.nn.silu(gate) * up).astype(BF16)
        acc = acc + jnp.dot(act, wd_ref[c0:c0 + n, :], preferred_element_type=F32)
    o_ref[...] = _rms(acc, gf_ref[...]) if final else acc


def _ffn(x, g_ffn, w_gu, w_down, g_final, layer, *, final, tm=512, chunk=1536):
    t = x.shape[0]
    row = pl.BlockSpec((tm, D_MODEL), lambda i: (i, 0))
    return pl.pallas_call(
        functools.partial(_ffn_kernel, chunk=chunk, final=final),
        grid=(t // tm,),
        in_specs=[row,
                  pl.BlockSpec((None, 1, D_MODEL), lambda i: (layer, 0, 0)),
                  _resident((None, D_MODEL, 2 * D_FF), lambda i: (layer, 0, 0)),
                  _resident((None, D_FF, D_MODEL), lambda i: (layer, 0, 0)),
                  pl.BlockSpec((1, D_MODEL), lambda i: (0, 0))],
        out_specs=row,
        out_shape=jax.ShapeDtypeStruct((t, D_MODEL), F32),
        compiler_params=_params(56, ("parallel",)),
        name="ffn",
    )(x, g_ffn, w_gu, w_down, g_final)


@jax.jit
def _trunk(x, w_in, conv_w, w_a_out, w_pool, pool_scale, w_attn_out, attn_sink, w_o,
           g_mix, g_ffn, w_gu, w_down, rel_bias, g_final):
    batch, seq, d = x.shape
    depth = w_in.shape[0]
    xf = x.reshape(batch * seq, d)

    w_in_b = w_in.astype(BF16)
    w_a_out_b = w_a_out.astype(BF16)
    w_pool_b = w_pool.astype(BF16)
    w_attn_out_b = w_attn_out.astype(BF16)
    w_o_b = w_o.astype(BF16)
    w_gu_b = w_gu.astype(BF16)
    w_down_b = w_down.astype(BF16)
    conv_w3 = conv_w.reshape(depth, 3, d)
    pool_scale3 = pool_scale.reshape(depth, 1, d)
    g_mix3 = g_mix.reshape(depth, 1, d)
    g_ffn3 = g_ffn.reshape(depth, 1, d)
    g_final2 = g_final.reshape(1, d)
    bias = _bias_table(rel_bias)

    for l in range(depth):
        proj = _inproj(xf, g_mix3, w_in_b, l)
        attn = _attention(proj, bias, attn_sink, l, seq=seq)
        x1 = _mixer_merge(proj, attn, xf, conv_w3, w_pool_b, pool_scale3, w_a_out_b,
                          w_attn_out_b, w_o_b, l, seq=seq)
        xf = _ffn(x1, g_ffn3, w_gu_b, w_down_b, g_final2, l, final=(l == depth - 1))
    return xf.reshape(batch, seq, d)


def kernel(x, w_in, conv_w, w_a_out, w_pool, pool_scale, w_attn_out, attn_sink, w_o, g_mix,
           g_ffn, w_gu, w_down, rel_bias, g_final):
    return _trunk(x, w_in, conv_w, w_a_out, w_pool, pool_scale, w_attn_out, attn_sink, w_o,
                  g_mix, g_ffn, w_gu, w_down, rel_bias, g_final)
```

```python
import functools
import math

import jax
import jax.numpy as jnp
from jax import lax
from jax.experimental import pallas as pl
from jax.experimental.pallas import tpu as pltpu

F32 = jnp.float32
BF16 = jnp.bfloat16

D_MODEL = 1024
N_HEADS = 16
N_KV_HEADS = 4
HEAD_DIM = 64
GROUP = N_HEADS // N_KV_HEADS
WINDOW = 128
BLOCK = 128
N_BUCKETS = 32
MAX_DISTANCE = 128
POOL_GROUPS = 4
POOL_CG = D_MODEL // POOL_GROUPS
D_FF = 2816
EPS = 1e-6
NEG_INF = -1e30

COL_B, COL_C, COL_X, COL_U, COL_Q = 0, 1024, 2048, 3072, 4096
COL_K, COL_V = 5120, 5376
COL_GA, COL_GP, COL_GT = 5632, 6656, 7680
IN_TOTAL = 8704
GATE_W = 512

LANES = 128
MXU_W = 256
SUM_ROWS = 16
CONV_HALO = 16
POOL_HALO = 64
POOL_WINDOWS = (2, 4, 8, 16)
MIB = 1024 * 1024

VAR_MID, VAR_FIRST, VAR_LAST = 0, 1, 2


def _params(vmem_mib, sem):
    return pltpu.CompilerParams(dimension_semantics=sem, vmem_limit_bytes=vmem_mib * MIB)


def _resident(shape, index_map):
    return pl.BlockSpec(shape, index_map, pipeline_mode=pl.Buffered(1))


def _rms(x, g):
    ms = jnp.mean(x * x, axis=-1, keepdims=True)
    return (x * lax.rsqrt(ms + EPS)) * g


def _col_chunks(total, width):
    assert total % MXU_W == 0 and width % MXU_W == 0
    return [(c, min(width, total - c)) for c in range(0, total, width)]


def _inproj_kernel(x_ref, g_ref, w_ref, o_ref, *, chunk):
    h = _rms(x_ref[...], g_ref[...]).astype(BF16)
    for c0, n in _col_chunks(IN_TOTAL, chunk):
        o_ref[:, c0:c0 + n] = jnp.dot(h, w_ref[:, c0:c0 + n],
                                      preferred_element_type=F32).astype(BF16)


def _inproj(x, g, w, layer, *, tm=512, chunk=1024):
    t = x.shape[0]
    return pl.pallas_call(
        functools.partial(_inproj_kernel, chunk=chunk),
        grid=(t // tm,),
        in_specs=[
            pl.BlockSpec((tm, D_MODEL), lambda i: (i, 0)),
            pl.BlockSpec((None, 1, D_MODEL), lambda i: (layer, 0, 0)),
            _resident((None, D_MODEL, IN_TOTAL), lambda i: (layer, 0, 0)),
        ],
        out_specs=pl.BlockSpec((tm, IN_TOTAL), lambda i: (i, 0)),
        out_shape=jax.ShapeDtypeStruct((t, IN_TOTAL), BF16),
        compiler_params=_params(48, ("parallel",)),
        name="inproj",
    )(x, g, w)


def _t5_bucket(rel):
    half = N_BUCKETS // 2
    max_exact = half // 2
    ret = jnp.where(rel > 0, half, 0)
    n = jnp.abs(rel)
    nf = jnp.maximum(n, 1).astype(jnp.float32)
    large = max_exact + (jnp.log(nf / max_exact) / math.log(MAX_DISTANCE / max_exact)
                         * (half - max_exact)).astype(jnp.int32)
    large = jnp.minimum(large, half - 1)
    return ret + jnp.where(n < max_exact, n, large)


def _bucket_map():
    kj = jnp.arange(3 * BLOCK)[:, None]
    qi = jnp.arange(BLOCK)[None, :]
    rel = kj - BLOCK - qi
    return jnp.where(jnp.abs(rel) <= WINDOW, _t5_bucket(rel), -1).astype(jnp.int32)


def _bias_kernel(relb_ref, bk_ref, o_ref):
    h = pl.program_id(0)
    bk = bk_ref[...]
    acc = jnp.full(bk.shape, NEG_INF, F32)
    for b in range(N_BUCKETS):
        acc = jnp.where(bk == b, relb_ref[b, h], acc)
    key = lax.broadcasted_iota(jnp.int32, bk.shape, 0)
    o_ref[VAR_MID] = acc
    o_ref[VAR_FIRST] = jnp.where(key < BLOCK, NEG_INF, acc)
    o_ref[VAR_LAST] = jnp.where(key >= 2 * BLOCK, NEG_INF, acc)


def _bias_table(rel_bias):
    return pl.pallas_call(
        _bias_kernel,
        grid=(N_HEADS,),
        in_specs=[
            pl.BlockSpec(memory_space=pltpu.SMEM),
            pl.BlockSpec((3 * BLOCK, BLOCK), lambda h: (0, 0)),
        ],
        out_specs=pl.BlockSpec((3, None, 3 * BLOCK, BLOCK), lambda h: (0, h, 0, 0)),
        out_shape=jax.ShapeDtypeStruct((3, N_HEADS, 3 * BLOCK, BLOCK), F32),
        compiler_params=_params(16, ("arbitrary",)),
        name="bias_table",
    )(rel_bias, _bucket_map())


def _attn_kernel(sink_ref, q_ref, kc_ref, kp_ref, kn_ref, vc_ref, vp_ref, vn_ref, bias_ref,
                 o_ref, klo_ref, khi_ref, vt_ref, *, layer, tq, blocks_per_seq):
    i = pl.program_id(0)
    nqb = tq // BLOCK
    assert 2 * HEAD_DIM == LANES and blocks_per_seq >= 2

    for pair in range(N_KV_HEADS // 2):
        r = 0
        for part in (kp_ref, kc_ref, kn_ref):
            n = part.shape[0]
            t = pltpu.bitcast(part[:, pair * LANES:(pair + 1) * LANES], jnp.uint32)
            moved = pltpu.roll(t, HEAD_DIM, 1)
            low = lax.broadcasted_iota(jnp.int32, t.shape, 1) < HEAD_DIM
            zero = jnp.zeros_like(t)
            for a, lo, hi in ((2 * pair, jnp.where(low, t, zero), jnp.where(low, zero, moved)),
                              (2 * pair + 1, jnp.where(low, moved, zero),
                               jnp.where(low, zero, t))):
                klo_ref[a, r:r + n, :] = pltpu.bitcast(lo, BF16)
                khi_ref[a, r:r + n, :] = pltpu.bitcast(hi, BF16)
            r += n
        c = 0
        for part in (vp_ref, vc_ref, vn_ref):
            for b0 in range(0, part.shape[0], BLOCK):
                blk = part[b0:b0 + BLOCK, pair * LANES:(pair + 1) * LANES].astype(F32)
                blk_t = blk.T.astype(BF16)
                vt_ref[2 * pair, 0:HEAD_DIM, c:c + BLOCK] = blk_t[0:HEAD_DIM, :]
                vt_ref[2 * pair + 1, 0:HEAD_DIM, c:c + BLOCK] = blk_t[HEAD_DIM:, :]
                c += BLOCK
    for a in range(N_KV_HEADS):
        vt_ref[a, HEAD_DIM:, :] = jnp.ones((SUM_ROWS, vt_ref.shape[2]), BF16)

    nt_dims = (((1,), (1,)), ((), ()))
    q_scale = jnp.asarray(HEAD_DIM ** -0.5, BF16)

    def scores(qb, a):
        q0 = qb * BLOCK
        c0 = a * GROUP * HEAD_DIM
        qt = jnp.concatenate([q_ref[q0:q0 + BLOCK, c0:c0 + LANES],
                              q_ref[q0:q0 + BLOCK, c0 + LANES:c0 + 2 * LANES]], axis=0)
        qt = qt * q_scale
        win = slice(q0, q0 + 3 * BLOCK)
        return [lax.dot_general(k_ref[a, win, :], qt, nt_dims, preferred_element_type=F32)
                for k_ref in (klo_ref, khi_ref)]

    def softmax(qb, a, s_par):
        n = lax.rem(i * nqb + qb, blocks_per_seq)
        var = jnp.where(n == 0, VAR_FIRST, jnp.where(n == blocks_per_seq - 1, VAR_LAST, VAR_MID))
        pts, sinks = [], []
        for par in range(2):
            p_t, e_t = [], []
            for tile in range(2):
                h = a * GROUP + 2 * tile + par
                sink = sink_ref[layer, h]
                sg = s_par[par][:, tile * BLOCK:(tile + 1) * BLOCK] + bias_ref[var, h]
                m = jnp.maximum(jnp.max(sg, axis=0, keepdims=True), sink)
                p_t.append(jnp.exp(sg - m).astype(BF16))
                e_t.append(jnp.exp(sink - m))
            pts.append(jnp.concatenate(p_t, axis=1))
            sinks.append(jnp.concatenate(e_t, axis=1))
        return pts, sinks

    def values(qb, a, pts, sinks):
        q0 = qb * BLOCK
        c0 = a * GROUP * HEAD_DIM
        v_t = vt_ref[a, :, q0:q0 + 3 * BLOCK]
        halves = []
        for par in range(2):
            o_full = jnp.dot(v_t, pts[par], preferred_element_type=F32)
            denom = o_full[HEAD_DIM:HEAD_DIM + 1, :] + sinks[par]
            halves.append(o_full[0:HEAD_DIM, :] / denom)
        o_t = jnp.concatenate(halves, axis=0)
        for tile in range(2):
            o_ref[q0:q0 + BLOCK, c0 + tile * LANES:c0 + (tile + 1) * LANES] = (
                o_t[:, tile * BLOCK:(tile + 1) * BLOCK].T.astype(BF16))

    units = [(qb, a) for qb in range(nqb) for a in range(N_KV_HEADS)]
    s_next = scores(*units[0])
    sm_prev = None
    for t in range(len(units) + 1):
        s_cur = s_next
        if t + 1 < len(units):
            s_next = scores(*units[t + 1])
        sm_cur = softmax(*units[t], s_cur) if t < len(units) else None
        if t >= 1:
            values(*units[t - 1], *sm_prev)
        sm_prev = sm_cur


def _attention(proj, bias, sink, layer, *, seq, tq=512):
    t = proj.shape[0]
    assert seq % tq == 0 and tq % BLOCK == 0
    nb = tq // BLOCK
    last_blk = t // BLOCK - 1
    kvw = N_KV_HEADS * HEAD_DIM
    kcol, vcol = COL_K // kvw, COL_V // kvw
    prev_map = lambda c: (lambda i: (jnp.maximum(i * nb - 1, 0), c))
    next_map = lambda c: (lambda i: (jnp.minimum((i + 1) * nb, last_blk), c))
    k_scratch = pltpu.VMEM((N_KV_HEADS, tq + 2 * BLOCK, LANES), BF16)
    vt_scratch = pltpu.VMEM((N_KV_HEADS, HEAD_DIM + SUM_ROWS, tq + 2 * BLOCK), BF16)
    return pl.pallas_call(
        functools.partial(_attn_kernel, layer=layer, tq=tq, blocks_per_seq=seq // BLOCK),
        grid=(t // tq,),
        in_specs=[
            pl.BlockSpec(memory_space=pltpu.SMEM),
            pl.BlockSpec((tq, D_MODEL), lambda i: (i, COL_Q // D_MODEL)),
            pl.BlockSpec((tq, kvw), lambda i: (i, kcol)),
            pl.BlockSpec((BLOCK, kvw), prev_map(kcol)),
            pl.BlockSpec((BLOCK, kvw), next_map(kcol)),
            pl.BlockSpec((tq, kvw), lambda i: (i, vcol)),
            pl.BlockSpec((BLOCK, kvw), prev_map(vcol)),
            pl.BlockSpec((BLOCK, kvw), next_map(vcol)),
            _resident((3, N_HEADS, 3 * BLOCK, BLOCK), lambda i: (0, 0, 0, 0)),
        ],
        out_specs=pl.BlockSpec((tq, D_MODEL), lambda i: (i, 0)),
        out_shape=jax.ShapeDtypeStruct((t, D_MODEL), BF16),
        scratch_shapes=[k_scratch, k_scratch, vt_scratch],
        compiler_params=_params(48, ("parallel",)),
        name="window_attn",
    )(sink, proj, proj, proj, proj, proj, proj, proj, bias)


def _shift(x, k):
    n = x.shape[0]
    return pltpu.roll(x, k % n, 0)


def _pool_band(tm):
    t_i = jnp.arange(tm)[:, None]
    j_i = jnp.arange(tm + 2 * POOL_HALO)[None, :]
    d = j_i - POOL_HALO - t_i
    bands = [(d >= -(w // 2)) & (d <= w - 1 - w // 2) for w in POOL_WINDOWS]
    return jnp.stack(bands).astype(BF16)


def _mixer_merge_kernel(b_ref, c_ref, cp_ref, cn_ref, xc_ref, xp_ref, xn_ref,
                        u_ref, up_ref, un_ref, t_ref,
                        ga0_ref, ga1_ref, gp0_ref, gp1_ref, gt0_ref, gt1_ref, x_ref,
                        cw_ref, band_ref, wp_ref, ps_ref, wa_ref, wt_ref, wo_ref,
                        o_ref, *, tiles_per_seq):
    tm = b_ref.shape[0]
    seq = tm * tiles_per_seq
    n = lax.rem(pl.program_id(0), tiles_per_seq)
    first, last = n == 0, n == tiles_per_seq - 1
    cg = POOL_CG
    assert D_MODEL // cg == POOL_GROUPS

    def gate(lo_ref, hi_ref):
        g = jnp.concatenate([lo_ref[...], hi_ref[...]], axis=1)
        return jax.nn.sigmoid(g.astype(F32))

    def conv_chunk(k):
        cols = slice(k * cg, (k + 1) * cg)

        def prod(c_r, x_r):
            return c_r[:, cols].astype(F32) * x_r[:, cols].astype(F32)

        u_ext = jnp.concatenate([jnp.where(first, 0.0, prod(cp_ref, xp_ref)),
                                 prod(c_ref, xc_ref),
                                 jnp.where(last, 0.0, prod(cn_ref, xn_ref))], axis=0)
        w = cw_ref[:, cols]
        y = _shift(u_ext, 1) * w[0:1, :] + u_ext * w[1:2, :] + _shift(u_ext, -1) * w[2:3, :]
        h = cp_ref.shape[0]
        return (b_ref[:, cols].astype(F32) * y[h:h + tm, :]).astype(BF16)

    def window_sum(g):
        cols = slice(g * cg, (g + 1) * cg)
        prev = jnp.where(first, 0.0, up_ref[:, cols].astype(F32)).astype(BF16)
        nxt = jnp.where(last, 0.0, un_ref[:, cols].astype(F32)).astype(BF16)
        ctx = jnp.concatenate([prev, u_ref[:, cols], nxt], axis=0)
        return jnp.dot(band_ref[g], ctx, preferred_element_type=F32)

    t_abs = n * tm + lax.broadcasted_iota(jnp.int32, (tm, 1), 0)

    def pooled(g, s):
        cols = slice(g * cg, (g + 1) * cg)
        win = POOL_WINDOWS[g]
        lo_off, hi_off = win // 2, win - 1 - win // 2
        cnt = jnp.minimum(t_abs + hi_off, seq - 1) - jnp.maximum(t_abs - lo_off, 0) + 1
        p = (s / cnt.astype(F32) - u_ref[:, cols].astype(F32)).astype(BF16)
        return jnp.dot(p, wp_ref[g], preferred_element_type=F32) * ps_ref[:, cols]

    yt = gate(gt0_ref, gt1_ref) * jnp.dot(t_ref[...], wt_ref[...], preferred_element_type=F32)
    sums = [window_sum(g) for g in range(POOL_GROUPS)]
    ya, yps = None, []
    for k in range(POOL_GROUPS):
        part = jnp.dot(conv_chunk(k), wa_ref[k * cg:(k + 1) * cg, :],
                       preferred_element_type=F32)
        ya = part if ya is None else ya + part
        yps.append(pooled(k, sums[k]))
    merged = (gate(ga0_ref, ga1_ref) * ya
              + gate(gp0_ref, gp1_ref) * jnp.concatenate(yps, axis=1)
              + yt)
    o_ref[...] = x_ref[...] + jnp.dot(merged.astype(BF16), wo_ref[...],
                                      preferred_element_type=F32)


def _mixer_merge(proj, attn, x, conv_w, w_pool, pool_scale, w_a_out, w_attn_out, w_o, layer,
                 *, seq, tm=512):
    t = x.shape[0]
    assert seq % tm == 0 and tm % POOL_HALO == 0 and tm % CONV_HALO == 0

    def row(col_block):
        return pl.BlockSpec((tm, D_MODEL), lambda i: (i, col_block))

    def halo(rows, col_block):
        per_tile, last_blk = tm // rows, t // rows - 1
        return (pl.BlockSpec((rows, D_MODEL),
                             lambda i: (jnp.maximum(i * per_tile - 1, 0), col_block)),
                pl.BlockSpec((rows, D_MODEL),
                             lambda i: (jnp.minimum((i + 1) * per_tile, last_blk), col_block)))

    def gate(col):
        def spec(col_block):
            return pl.BlockSpec((tm, GATE_W), lambda i: (i, col_block))
        return [spec(col // GATE_W + half) for half in range(D_MODEL // GATE_W)]

    wspec = _resident((None, D_MODEL, D_MODEL), lambda i: (layer, 0, 0))
    band = _pool_band(tm)
    c_blk, x_blk, u_blk = COL_C // D_MODEL, COL_X // D_MODEL, COL_U // D_MODEL
    return pl.pallas_call(
        functools.partial(_mixer_merge_kernel, tiles_per_seq=seq // tm),
        grid=(t // tm,),
        in_specs=[row(COL_B // D_MODEL),
                  row(c_blk), *halo(CONV_HALO, c_blk),
                  row(x_blk), *halo(CONV_HALO, x_blk),
                  row(u_blk), *halo(POOL_HALO, u_blk),
                  row(0),
                  *gate(COL_GA), *gate(COL_GP), *gate(COL_GT),
                  row(0),
                  pl.BlockSpec((None, 3, D_MODEL), lambda i: (layer, 0, 0)),
                  _resident(band.shape, lambda i: (0, 0, 0)),
                  _resident((None, POOL_GROUPS, POOL_CG, POOL_CG), lambda i: (layer, 0, 0, 0)),
                  pl.BlockSpec((None, 1, D_MODEL), lambda i: (layer, 0, 0)),
                  wspec, wspec, wspec],
        out_specs=row(0),
        out_shape=jax.ShapeDtypeStruct((t, D_MODEL), F32),
        compiler_params=_params(56, ("parallel",)),
        name="mixer_merge",
    )(proj, proj, proj, proj, proj, proj, proj, proj, proj, proj, attn,
      proj, proj, proj, proj, proj, proj, x,
      conv_w, band, w_pool, pool_scale, w_a_out, w_attn_out, w_o)


def _ffn_kernel(x_ref, g_ref, wgu_ref, wd_ref, gf_ref, o_ref, *, chunk, final):
    x = x_ref[...]
    h = _rms(x, g_ref[...]).astype(BF16)
    acc = x
    for c0, n in _col_chunks(D_FF, chunk):
        gate = jnp.dot(h, wgu_ref[:, c0:c0 + n], preferred_element_type=F32)
        up = jnp.dot(h, wgu_ref[:, D_FF + c0:D_FF + c0 + n], preferred_element_type=F32)
        act = (jax.nn.silu(gate) * up).astype(BF16)
        acc = acc + jnp.dot(act, wd_ref[c0:c0 + n, :], preferred_element_type=F32)
    o_ref[...] = _rms(acc, gf_ref[...]) if final else acc


def _ffn(x, g_ffn, w_gu, w_down, g_final, layer, *, final, tm=1024, chunk=768):
    t = x.shape[0]
    row = pl.BlockSpec((tm, D_MODEL), lambda i: (i, 0))
    return pl.pallas_call(
        functools.partial(_ffn_kernel, chunk=chunk, final=final),
        grid=(t // tm,),
        in_specs=[row,
                  pl.BlockSpec((None, 1, D_MODEL), lambda i: (layer, 0, 0)),
                  _resident((None, D_MODEL, 2 * D_FF), lambda i: (layer, 0, 0)),
                  _resident((None, D_FF, D_MODEL), lambda i: (layer, 0, 0)),
                  pl.BlockSpec((1, D_MODEL), lambda i: (0, 0))],
        out_specs=row,
        out_shape=jax.ShapeDtypeStruct((t, D_MODEL), F32),
        compiler_params=_params(56, ("parallel",)),
        name="ffn",
    )(x, g_ffn, w_gu, w_down, g_final)


@jax.jit
def _trunk(x, w_in, conv_w, w_a_out, w_pool, pool_scale, w_attn_out, attn_sink, w_o,
           g_mix, g_ffn, w_gu, w_down, rel_bias, g_final):
    batch, seq, d = x.shape
    depth = w_in.shape[0]
    xf = x.reshape(batch * seq, d)

    w_in_b = w_in.astype(BF16)
    w_a_out_b = w_a_out.astype(BF16)
    w_pool_b = w_pool.astype(BF16)
    w_attn_out_b = w_attn_out.astype(BF16)
    w_o_b = w_o.astype(BF16)
    w_gu_b = w_gu.astype(BF16)
    w_down_b = w_down.astype(BF16)
    conv_w3 = conv_w.reshape(depth, 3, d)
    pool_scale3 = pool_scale.reshape(depth, 1, d)
    g_mix3 = g_mix.reshape(depth, 1, d)
    g_ffn3 = g_ffn.reshape(depth, 1, d)
    g_final2 = g_final.reshape(1, d)
    bias = _bias_table(rel_bias)

    for l in range(depth):
        proj = _inproj(xf, g_mix3, w_in_b, l)
        attn = _attention(proj, bias, attn_sink, l, seq=seq)
        x1 = _mixer_merge(proj, attn, xf, conv_w3, w_pool_b, pool_scale3, w_a_out_b,
                          w_attn_out_b, w_o_b, l, seq=seq)
        xf = _ffn(x1, g_ffn3, w_gu_b, w_down_b, g_final2, l, final=(l == depth - 1))
    return xf.reshape(batch, seq, d)


def kernel(x, w_in, conv_w, w_a_out, w_pool, pool_scale, w_attn_out, attn_sink, w_o, g_mix,
           g_ffn, w_gu, w_down, rel_bias, g_final):
    return _trunk(x, w_in, conv_w, w_a_out, w_pool, pool_scale, w_attn_out, attn_sink, w_o,
                  g_mix, g_ffn, w_gu, w_down, rel_bias, g_final)
```

```python
import functools
import math

import jax
import jax.numpy as jnp
from jax import lax
from jax.experimental import pallas as pl
from jax.experimental.pallas import tpu as pltpu

F32 = jnp.float32
BF16 = jnp.bfloat16

D_MODEL = 1024
N_HEADS = 16
N_KV_HEADS = 4
HEAD_DIM = 64
GROUP = N_HEADS // N_KV_HEADS
WINDOW = 128
BLOCK = 128
N_BUCKETS = 32
MAX_DISTANCE = 128
POOL_GROUPS = 4
POOL_CG = D_MODEL // POOL_GROUPS
D_FF = 2816
EPS = 1e-6
NEG_INF = -1e30

W_COL_B, W_COL_C, W_COL_X, W_COL_REST = 0, 1024, 2048, 3072
IN_TOTAL = 8704
COL_A, COL_U, COL_Q = 0, 1024, 2048
COL_K, COL_V = 3072, 3328
COL_GA, COL_GP, COL_GT = 3584, 4608, 5632
PROJ_W = D_MODEL + IN_TOTAL - W_COL_REST
GATE_W = 512

LANES = 128
MXU_W = 256
SUM_ROWS = 16
CONV_HALO = 16
POOL_HALO = 64
POOL_WINDOWS = (2, 4, 8, 16)
MIB = 1024 * 1024

VAR_MID, VAR_FIRST, VAR_LAST = 0, 1, 2


def _params(vmem_mib, sem):
    return pltpu.CompilerParams(dimension_semantics=sem, vmem_limit_bytes=vmem_mib * MIB)


def _resident(shape, index_map):
    return pl.BlockSpec(shape, index_map, pipeline_mode=pl.Buffered(1))


def _rms(x, g):
    ms = jnp.mean(x * x, axis=-1, keepdims=True)
    return (x * lax.rsqrt(ms + EPS)) * g


def _col_chunks(total, width):
    assert total % MXU_W == 0 and width % MXU_W == 0
    return [(c, min(width, total - c)) for c in range(0, total, width)]


def _shift(x, k):
    n = x.shape[0]
    return pltpu.roll(x, k % n, 0)


def _inproj_kernel(x_ref, xp_ref, xn_ref, g_ref, cw_ref, w_ref, o_ref,
                   *, chunk, conv_chunk, tiles_per_seq):
    tm = x_ref.shape[0]
    halo = xp_ref.shape[0]
    n = lax.rem(pl.program_id(0), tiles_per_seq)
    g = g_ref[...]
    h = _rms(x_ref[...], g).astype(BF16)
    h_prev = jnp.where(n == 0, 0.0, _rms(xp_ref[...], g)).astype(BF16)
    h_next = jnp.where(n == tiles_per_seq - 1, 0.0, _rms(xn_ref[...], g)).astype(BF16)
    h_ext = jnp.concatenate([h_prev, h, h_next], axis=0)

    for c0, nc in _col_chunks(D_MODEL, conv_chunk):
        def proj(lhs, col):
            return jnp.dot(lhs, w_ref[:, col + c0:col + c0 + nc], preferred_element_type=F32)

        u = proj(h_ext, W_COL_C) * proj(h_ext, W_COL_X)
        w = cw_ref[:, c0:c0 + nc]
        y = _shift(u, 1) * w[0:1, :] + u * w[1:2, :] + _shift(u, -1) * w[2:3, :]
        o_ref[:, COL_A + c0:COL_A + c0 + nc] = (
            proj(h, W_COL_B) * y[halo:halo + tm, :]).astype(BF16)

    for c0, nc in _col_chunks(IN_TOTAL - W_COL_REST, chunk):
        o_ref[:, COL_U + c0:COL_U + c0 + nc] = jnp.dot(
            h, w_ref[:, W_COL_REST + c0:W_COL_REST + c0 + nc],
            preferred_element_type=F32).astype(BF16)


def _inproj(x, g, conv_w, w, layer, *, seq, tm=512, chunk=1024, conv_chunk=256):
    t = x.shape[0]
    assert seq % tm == 0 and tm % CONV_HALO == 0
    per_tile, last_blk = tm // CONV_HALO, t // CONV_HALO - 1
    return pl.pallas_call(
        functools.partial(_inproj_kernel, chunk=chunk, conv_chunk=conv_chunk,
                          tiles_per_seq=seq // tm),
        grid=(t // tm,),
        in_specs=[
            pl.BlockSpec((tm, D_MODEL), lambda i: (i, 0)),
            pl.BlockSpec((CONV_HALO, D_MODEL), lambda i: (jnp.maximum(i * per_tile - 1, 0), 0)),
            pl.BlockSpec((CONV_HALO, D_MODEL),
                         lambda i: (jnp.minimum((i + 1) * per_tile, last_blk), 0)),
            pl.BlockSpec((None, 1, D_MODEL), lambda i: (layer, 0, 0)),
            pl.BlockSpec((None, 3, D_MODEL), lambda i: (layer, 0, 0)),
            _resident((None, D_MODEL, IN_TOTAL), lambda i: (layer, 0, 0)),
        ],
        out_specs=pl.BlockSpec((tm, PROJ_W), lambda i: (i, 0)),
        out_shape=jax.ShapeDtypeStruct((t, PROJ_W), BF16),
        compiler_params=_params(52, ("parallel",)),
        name="inproj",
    )(x, x, x, g, conv_w, w)


def _t5_bucket(rel):
    half = N_BUCKETS // 2
    max_exact = half // 2
    ret = jnp.where(rel > 0, half, 0)
    n = jnp.abs(rel)
    nf = jnp.maximum(n, 1).astype(jnp.float32)
    large = max_exact + (jnp.log(nf / max_exact) / math.log(MAX_DISTANCE / max_exact)
                         * (half - max_exact)).astype(jnp.int32)
    large = jnp.minimum(large, half - 1)
    return ret + jnp.where(n < max_exact, n, large)


def _bucket_map():
    kj = jnp.arange(3 * BLOCK)[:, None]
    qi = jnp.arange(BLOCK)[None, :]
    rel = kj - BLOCK - qi
    return jnp.where(jnp.abs(rel) <= WINDOW, _t5_bucket(rel), -1).astype(jnp.int32)


def _bias_kernel(relb_ref, bk_ref, o_ref):
    h = pl.program_id(0)
    bk = bk_ref[...]
    acc = jnp.full(bk.shape, NEG_INF, F32)
    for b in range(N_BUCKETS):
        acc = jnp.where(bk == b, relb_ref[b, h], acc)
    key = lax.broadcasted_iota(jnp.int32, bk.shape, 0)
    o_ref[VAR_MID] = acc
    o_ref[VAR_FIRST] = jnp.where(key < BLOCK, NEG_INF, acc)
    o_ref[VAR_LAST] = jnp.where(key >= 2 * BLOCK, NEG_INF, acc)


def _bias_table(rel_bias):
    return pl.pallas_call(
        _bias_kernel,
        grid=(N_HEADS,),
        in_specs=[
            pl.BlockSpec(memory_space=pltpu.SMEM),
            pl.BlockSpec((3 * BLOCK, BLOCK), lambda h: (0, 0)),
        ],
        out_specs=pl.BlockSpec((3, None, 3 * BLOCK, BLOCK), lambda h: (0, h, 0, 0)),
        out_shape=jax.ShapeDtypeStruct((3, N_HEADS, 3 * BLOCK, BLOCK), F32),
        compiler_params=_params(16, ("arbitrary",)),
        name="bias_table",
    )(rel_bias, _bucket_map())


def _attn_kernel(sink_ref, q_ref, kc_ref, kp_ref, kn_ref, vc_ref, vp_ref, vn_ref, bias_ref,
                 o_ref, klo_ref, khi_ref, vt_ref, *, layer, tq, blocks_per_seq):
    i = pl.program_id(0)
    nqb = tq // BLOCK
    assert 2 * HEAD_DIM == LANES and blocks_per_seq >= 2

    for pair in range(N_KV_HEADS // 2):
        r = 0
        for part in (kp_ref, kc_ref, kn_ref):
            n = part.shape[0]
            t = pltpu.bitcast(part[:, pair * LANES:(pair + 1) * LANES], jnp.uint32)
            moved = pltpu.roll(t, HEAD_DIM, 1)
            low = lax.broadcasted_iota(jnp.int32, t.shape, 1) < HEAD_DIM
            zero = jnp.zeros_like(t)
            for a, lo, hi in ((2 * pair, jnp.where(low, t, zero), jnp.where(low, zero, moved)),
                              (2 * pair + 1, jnp.where(low, moved, zero),
                               jnp.where(low, zero, t))):
                klo_ref[a, r:r + n, :] = pltpu.bitcast(lo, BF16)
                khi_ref[a, r:r + n, :] = pltpu.bitcast(hi, BF16)
            r += n
        c = 0
        for part in (vp_ref, vc_ref, vn_ref):
            for b0 in range(0, part.shape[0], BLOCK):
                blk = part[b0:b0 + BLOCK, pair * LANES:(pair + 1) * LANES].astype(F32)
                blk_t = blk.T.astype(BF16)
                vt_ref[2 * pair, 0:HEAD_DIM, c:c + BLOCK] = blk_t[0:HEAD_DIM, :]
                vt_ref[2 * pair + 1, 0:HEAD_DIM, c:c + BLOCK] = blk_t[HEAD_DIM:, :]
                c += BLOCK
    for a in range(N_KV_HEADS):
        vt_ref[a, HEAD_DIM:, :] = jnp.ones((SUM_ROWS, vt_ref.shape[2]), BF16)

    nt_dims = (((1,), (1,)), ((), ()))
    q_scale = jnp.asarray(HEAD_DIM ** -0.5, BF16)

    def scores(qb, a):
        q0 = qb * BLOCK
        c0 = a * GROUP * HEAD_DIM
        qt = jnp.concatenate([q_ref[q0:q0 + BLOCK, c0:c0 + LANES],
                              q_ref[q0:q0 + BLOCK, c0 + LANES:c0 + 2 * LANES]], axis=0)
        qt = qt * q_scale
        win = slice(q0, q0 + 3 * BLOCK)
        return [lax.dot_general(k_ref[a, win, :], qt, nt_dims, preferred_element_type=F32)
                for k_ref in (klo_ref, khi_ref)]

    def softmax(qb, a, s_par):
        n = lax.rem(i * nqb + qb, blocks_per_seq)
        var = jnp.where(n == 0, VAR_FIRST, jnp.where(n == blocks_per_seq - 1, VAR_LAST, VAR_MID))
        pts, sinks = [], []
        for par in range(2):
            p_t, e_t = [], []
            for tile in range(2):
                h = a * GROUP + 2 * tile + par
                sink = sink_ref[layer, h]
                sg = s_par[par][:, tile * BLOCK:(tile + 1) * BLOCK] + bias_ref[var, h]
                m = jnp.maximum(jnp.max(sg, axis=0, keepdims=True), sink)
                p_t.append(jnp.exp(sg - m).astype(BF16))
                e_t.append(jnp.exp(sink - m))
            pts.append(jnp.concatenate(p_t, axis=1))
            sinks.append(jnp.concatenate(e_t, axis=1))
        return pts, sinks

    def values(qb, a, pts, sinks):
        q0 = qb * BLOCK
        c0 = a * GROUP * HEAD_DIM
        v_t = vt_ref[a, :, q0:q0 + 3 * BLOCK]
        halves = []
        for par in range(2):
            o_full = jnp.dot(v_t, pts[par], preferred_element_type=F32)
            denom = o_full[HEAD_DIM:HEAD_DIM + 1, :] + sinks[par]
            halves.append(o_full[0:HEAD_DIM, :] / denom)
        o_t = jnp.concatenate(halves, axis=0)
        for tile in range(2):
            o_ref[q0:q0 + BLOCK, c0 + tile * LANES:c0 + (tile + 1) * LANES] = (
                o_t[:, tile * BLOCK:(tile + 1) * BLOCK].T.astype(BF16))

    units = [(qb, a) for qb in range(nqb) for a in range(N_KV_HEADS)]
    s_next = scores(*units[0])
    sm_prev = None
    for t in range(len(units) + 1):
        s_cur = s_next
        if t + 1 < len(units):
            s_next = scores(*units[t + 1])
        sm_cur = softmax(*units[t], s_cur) if t < len(units) else None
        if t >= 1:
            values(*units[t - 1], *sm_prev)
        sm_prev = sm_cur


def _attention(proj, bias, sink, layer, *, seq, tq=512):
    t = proj.shape[0]
    assert seq % tq == 0 and tq % BLOCK == 0
    nb = tq // BLOCK
    last_blk = t // BLOCK - 1
    kvw = N_KV_HEADS * HEAD_DIM
    kcol, vcol = COL_K // kvw, COL_V // kvw
    prev_map = lambda c: (lambda i: (jnp.maximum(i * nb - 1, 0), c))
    next_map = lambda c: (lambda i: (jnp.minimum((i + 1) * nb, last_blk), c))
    k_scratch = pltpu.VMEM((N_KV_HEADS, tq + 2 * BLOCK, LANES), BF16)
    vt_scratch = pltpu.VMEM((N_KV_HEADS, HEAD_DIM + SUM_ROWS, tq + 2 * BLOCK), BF16)
    return pl.pallas_call(
        functools.partial(_attn_kernel, layer=layer, tq=tq, blocks_per_seq=seq // BLOCK),
        grid=(t // tq,),
        in_specs=[
            pl.BlockSpec(memory_space=pltpu.SMEM),
            pl.BlockSpec((tq, D_MODEL), lambda i: (i, COL_Q // D_MODEL)),
            pl.BlockSpec((tq, kvw), lambda i: (i, kcol)),
            pl.BlockSpec((BLOCK, kvw), prev_map(kcol)),
            pl.BlockSpec((BLOCK, kvw), next_map(kcol)),
            pl.BlockSpec((tq, kvw), lambda i: (i, vcol)),
            pl.BlockSpec((BLOCK, kvw), prev_map(vcol)),
            pl.BlockSpec((BLOCK, kvw), next_map(vcol)),
            _resident((3, N_HEADS, 3 * BLOCK, BLOCK), lambda i: (0, 0, 0, 0)),
        ],
        out_specs=pl.BlockSpec((tq, D_MODEL), lambda i: (i, 0)),
        out_shape=jax.ShapeDtypeStruct((t, D_MODEL), BF16),
        scratch_shapes=[k_scratch, k_scratch, vt_scratch],
        compiler_params=_params(48, ("parallel",)),
        name="window_attn",
    )(sink, proj, proj, proj, proj, proj, proj, proj, bias)


def _pool_band(tm):
    t_i = jnp.arange(tm)[:, None]
    j_i = jnp.arange(tm + 2 * POOL_HALO)[None, :]
    d = j_i - POOL_HALO - t_i
    bands = [(d >= -(w // 2)) & (d <= w - 1 - w // 2) for w in POOL_WINDOWS]
    return jnp.stack(bands).astype(BF16)


def _mixer_merge_kernel(a_ref, u_ref, up_ref, un_ref, t_ref,
                        ga0_ref, ga1_ref, gp0_ref, gp1_ref, gt0_ref, gt1_ref, x_ref,
                        band_ref, wp_ref, ps_ref, wa_ref, wt_ref, wo_ref,
                        o_ref, *, tiles_per_seq):
    tm = a_ref.shape[0]
    seq = tm * tiles_per_seq
    n = lax.rem(pl.program_id(0), tiles_per_seq)
    first, last = n == 0, n == tiles_per_seq - 1
    cg = POOL_CG
    assert D_MODEL // cg == POOL_GROUPS

    def gate(lo_ref, hi_ref):
        g = jnp.concatenate([lo_ref[...], hi_ref[...]], axis=1)
        return jax.nn.sigmoid(g.astype(F32))

    def window_sum(g):
        cols = slice(g * cg, (g + 1) * cg)
        prev = jnp.where(first, 0.0, up_ref[:, cols].astype(F32)).astype(BF16)
        nxt = jnp.where(last, 0.0, un_ref[:, cols].astype(F32)).astype(BF16)
        ctx = jnp.concatenate([prev, u_ref[:, cols], nxt], axis=0)
        return jnp.dot(band_ref[g], ctx, preferred_element_type=F32)

    t_abs = n * tm + lax.broadcasted_iota(jnp.int32, (tm, 1), 0)

    def pooled(g, s):
        cols = slice(g * cg, (g + 1) * cg)
        win = POOL_WINDOWS[g]
        lo_off, hi_off = win // 2, win - 1 - win // 2
        cnt = jnp.minimum(t_abs + hi_off, seq - 1) - jnp.maximum(t_abs - lo_off, 0) + 1
        p = (s / cnt.astype(F32) - u_ref[:, cols].astype(F32)).astype(BF16)
        return jnp.dot(p, wp_ref[g], preferred_element_type=F32) * ps_ref[:, cols]

    sums = [window_sum(g) for g in range(POOL_GROUPS)]
    ya = jnp.dot(a_ref[...], wa_ref[...], preferred_element_type=F32)
    yp = jnp.concatenate([pooled(g, sums[g]) for g in range(POOL_GROUPS)], axis=1)
    merged = gate(ga0_ref, ga1_ref) * ya + gate(gp0_ref, gp1_ref) * yp
    yt = jnp.dot(t_ref[...], wt_ref[...], preferred_element_type=F32)
    merged = merged + gate(gt0_ref, gt1_ref) * yt
    o_ref[...] = x_ref[...] + jnp.dot(merged.astype(BF16), wo_ref[...],
                                      preferred_element_type=F32)


def _mixer_merge(proj, attn, x, w_pool, pool_scale, w_a_out, w_attn_out, w_o, layer,
                 *, seq, tm=512):
    t = x.shape[0]
    assert seq % tm == 0 and tm % POOL_HALO == 0

    def row(col_block):
        return pl.BlockSpec((tm, D_MODEL), lambda i: (i, col_block))

    def halo(rows, col_block):
        per_tile, last_blk = tm // rows, t // rows - 1
        return (pl.BlockSpec((rows, D_MODEL),
                             lambda i: (jnp.maximum(i * per_tile - 1, 0), col_block)),
                pl.BlockSpec((rows, D_MODEL),
                             lambda i: (jnp.minimum((i + 1) * per_tile, last_blk), col_block)))

    def gate(col):
        def spec(col_block):
            return pl.BlockSpec((tm, GATE_W), lambda i: (i, col_block))
        return [spec(col // GATE_W + half) for half in range(D_MODEL // GATE_W)]

    wspec = _resident((None, D_MODEL, D_MODEL), lambda i: (layer, 0, 0))
    band = _pool_band(tm)
    u_blk = COL_U // D_MODEL
    return pl.pallas_call(
        functools.partial(_mixer_merge_kernel, tiles_per_seq=seq // tm),
        grid=(t // tm,),
        in_specs=[row(COL_A // D_MODEL),
                  row(u_blk), *halo(POOL_HALO, u_blk),
                  row(0),
                  *gate(COL_GA), *gate(COL_GP), *gate(COL_GT),
                  row(0),
                  _resident(band.shape, lambda i: (0, 0, 0)),
                  _resident((None, POOL_GROUPS, POOL_CG, POOL_CG), lambda i: (layer, 0, 0, 0)),
                  pl.BlockSpec((None, 1, D_MODEL), lambda i: (layer, 0, 0)),
                  wspec, wspec, wspec],
        out_specs=row(0),
        out_shape=jax.ShapeDtypeStruct((t, D_MODEL), F32),
        compiler_params=_params(56, ("parallel",)),
        name="mixer_merge",
    )(proj, proj, proj, proj, attn,
      proj, proj, proj, proj, proj, proj, x,
      band, w_pool, pool_scale, w_a_out, w_attn_out, w_o)


def _ffn_kernel(x_ref, g_ref, wgu_ref, wd_ref, gf_ref, o_ref, *, chunk, final):
    x = x_ref[...]
    h = _rms(x, g_ref[...]).astype(BF16)
    acc = x
    for c0, n in _col_chunks(D_FF, chunk):
        gate = jnp.dot(h, wgu_ref[:, c0:c0 + n], preferred_element_type=F32)
        up = jnp.dot(h, wgu_ref[:, D_FF + c0:D_FF + c0 + n], preferred_element_type=F32)
        act = (jax.nn.silu(gate) * up).astype(BF16)
        acc = acc + jnp.dot(act, wd_ref[c0:c0 + n, :], preferred_element_type=F32)
    o_ref[...] = _rms(acc, gf_ref[...]) if final else acc


def _ffn(x, g_ffn, w_gu, w_down, g_final, layer, *, final, tm=512, chunk=1536):
    t = x.shape[0]
    row = pl.BlockSpec((tm, D_MODEL), lambda i: (i, 0))
    return pl.pallas_call(
        functools.partial(_ffn_kernel, chunk=chunk, final=final),
        grid=(t // tm,),
        in_specs=[row,
                  pl.BlockSpec((None, 1, D_MODEL), lambda i: (layer, 0, 0)),
                  _resident((None, D_MODEL, 2 * D_FF), lambda i: (layer, 0, 0)),
                  _resident((None, D_FF, D_MODEL), lambda i: (layer, 0, 0)),
                  pl.BlockSpec((1, D_MODEL), lambda i: (0, 0))],
        out_specs=row,
        out_shape=jax.ShapeDtypeStruct((t, D_MODEL), F32),
        compiler_params=_params(56, ("parallel",)),
        name="ffn",
    )(x, g_ffn, w_gu, w_down, g_final)


@jax.jit
def _trunk(x, w_in, conv_w, w_a_out, w_pool, pool_scale, w_attn_out, attn_sink, w_o,
           g_mix, g_ffn, w_gu, w_down, rel_bias, g_final):
    batch, seq, d = x.shape
    depth = w_in.shape[0]
    xf = x.reshape(batch * seq, d)

    w_in_b = w_in.astype(BF16)
    w_a_out_b = w_a_out.astype(BF16)
    w_pool_b = w_pool.astype(BF16)
    w_attn_out_b = w_attn_out.astype(BF16)
    w_o_b = w_o.astype(BF16)
    w_gu_b = w_gu.astype(BF16)
    w_down_b = w_down.astype(BF16)
    conv_w3 = conv_w.reshape(depth, 3, d)
    pool_scale3 = pool_scale.reshape(depth, 1, d)
    g_mix3 = g_mix.reshape(depth, 1, d)
    g_ffn3 = g_ffn.reshape(depth, 1, d)
    g_final2 = g_final.reshape(1, d)
    bias = _bias_table(rel_bias)

    for l in range(depth):
        proj = _inproj(xf, g_mix3, conv_w3, w_in_b, l, seq=seq)
        attn = _attention(proj, bias, attn_sink, l, seq=seq)
        x1 = _mixer_merge(proj, attn, xf, w_pool_b, pool_scale3, w_a_out_b,
                          w_attn_out_b, w_o_b, l, seq=seq)
        xf = _ffn(x1, g_ffn3, w_gu_b, w_down_b, g_final2, l, final=(l == depth - 1))
    return xf.reshape(batch, seq, d)


def kernel(x, w_in, conv_w, w_a_out, w_pool, pool_scale, w_attn_out, attn_sink, w_o, g_mix,
           g_ffn, w_gu, w_down, rel_bias, g_final):
    return _trunk(x, w_in, conv_w, w_a_out, w_pool, pool_scale, w_attn_out, attn_sink, w_o,
                  g_mix, g_ffn, w_gu, w_down, rel_bias, g_final)
```

```python
import functools
import math

import jax
import jax.numpy as jnp
from jax import lax
from jax.experimental import pallas as pl
from jax.experimental.pallas import tpu as pltpu

F32 = jnp.float32
BF16 = jnp.bfloat16

D_MODEL = 1024
N_HEADS = 16
N_KV_HEADS = 4
HEAD_DIM = 64
GROUP = N_HEADS // N_KV_HEADS
WINDOW = 128
BLOCK = 128
N_BUCKETS = 32
MAX_DISTANCE = 128
POOL_GROUPS = 4
POOL_CG = D_MODEL // POOL_GROUPS
D_FF = 2816
EPS = 1e-6
NEG_INF = -1e30

W_COL_B, W_COL_C, W_COL_X, W_COL_REST = 0, 1024, 2048, 3072
IN_TOTAL = 8704
COL_A, COL_U, COL_Q = 0, 1024, 2048
COL_K, COL_V = 3072, 3328
COL_GA, COL_GP, COL_GT = 3584, 4608, 5632
PROJ_W = D_MODEL + IN_TOTAL - W_COL_REST
GATE_W = 512

LANES = 128
MXU_W = 256
ROW_TILE = 512
BF16_ROWS = 16
SUM_ROWS = BF16_ROWS
CONV_HALO = BF16_ROWS
POOL_HALO = 64
POOL_WINDOWS = (2, 4, 8, 16)
MIB = 1024 * 1024

VAR_MID, VAR_FIRST, VAR_LAST = 0, 1, 2


def _params(vmem_mib, sem):
    return pltpu.CompilerParams(dimension_semantics=sem, vmem_limit_bytes=vmem_mib * MIB)


def _resident(shape, index_map):
    return pl.BlockSpec(shape, index_map, pipeline_mode=pl.Buffered(1))


def _rms(x, g):
    ms = jnp.mean(x * x, axis=-1, keepdims=True)
    return (x * lax.rsqrt(ms + EPS)) * g


class _Cast:
    def __init__(self, weights, layer, rows):
        _, r, c = weights.shape
        assert r % rows == 0 and rows % BF16_ROWS == 0
        self.weights, self.layer, self.rows, self.r, self.c = weights, layer, rows, r, c

    def specs(self, n_steps):
        last = self.r // self.rows - 1
        assert last < n_steps
        layer = self.layer
        return (pl.BlockSpec((None, self.rows, self.c),
                             lambda i: (layer, jnp.minimum(i, last), 0)),
                pl.BlockSpec((self.rows, self.c), lambda i: (jnp.minimum(i, last), 0)),
                jax.ShapeDtypeStruct((self.r, self.c), BF16))


def _with_casts(body, n_in, n_casts):
    def kernel(*refs):
        own_in = refs[:n_in]
        cast_src = refs[n_in:n_in + n_casts]
        out = refs[n_in + n_casts]
        cast_dst = refs[n_in + n_casts + 1:n_in + 2 * n_casts + 1]
        scratch = refs[n_in + 2 * n_casts + 1:]
        for src, dst in zip(cast_src, cast_dst):
            dst[...] = src[...].astype(BF16)
        body(*own_in, out, *scratch)
    return kernel


def _call_with_casts(body, casts, *, grid, in_specs, out_spec, out_shape, operands, **kwargs):
    n_steps = grid[0]
    cast_specs = [c.specs(n_steps) for c in casts]
    outs = pl.pallas_call(
        _with_casts(body, len(in_specs), len(casts)),
        grid=grid,
        in_specs=list(in_specs) + [s[0] for s in cast_specs],
        out_specs=[out_spec] + [s[1] for s in cast_specs],
        out_shape=[out_shape] + [s[2] for s in cast_specs],
        **kwargs,
    )(*operands, *[c.weights for c in casts])
    return tuple(outs)


def _col_chunks(total, width):
    assert total % MXU_W == 0 and width % MXU_W == 0
    return [(c, min(width, total - c)) for c in range(0, total, width)]


def _shift(x, k):
    n = x.shape[0]
    return pltpu.roll(x, k % n, 0)


def _inproj_kernel(x_ref, xp_ref, xn_ref, g_ref, cw_ref, w_ref, o_ref,
                   *, chunk, conv_chunk, tiles_per_seq):
    tm = x_ref.shape[0]
    halo = xp_ref.shape[0]
    n = lax.rem(pl.program_id(0), tiles_per_seq)
    g = g_ref[...]
    h = _rms(x_ref[...], g).astype(BF16)
    h_prev = jnp.where(n == 0, 0.0, _rms(xp_ref[...], g)).astype(BF16)
    h_next = jnp.where(n == tiles_per_seq - 1, 0.0, _rms(xn_ref[...], g)).astype(BF16)
    h_ext = jnp.concatenate([h_prev, h, h_next], axis=0)

    for c0, nc in _col_chunks(D_MODEL, conv_chunk):
        def proj(lhs, col):
            return jnp.dot(lhs, w_ref[:, col + c0:col + c0 + nc], preferred_element_type=F32)

        u = proj(h_ext, W_COL_C) * proj(h_ext, W_COL_X)
        w = cw_ref[:, c0:c0 + nc]
        y = _shift(u, 1) * w[0:1, :] + u * w[1:2, :] + _shift(u, -1) * w[2:3, :]
        o_ref[:, COL_A + c0:COL_A + c0 + nc] = (
            proj(h, W_COL_B) * y[halo:halo + tm, :]).astype(BF16)

    for c0, nc in _col_chunks(IN_TOTAL - W_COL_REST, chunk):
        o_ref[:, COL_U + c0:COL_U + c0 + nc] = jnp.dot(
            h, w_ref[:, W_COL_REST + c0:W_COL_REST + c0 + nc],
            preferred_element_type=F32).astype(BF16)


def _inproj(x, g, conv_w, w, layer, casts, *, seq, tm=ROW_TILE, chunk=1024, conv_chunk=256):
    t = x.shape[0]
    assert seq % tm == 0 and tm % CONV_HALO == 0
    per_tile, last_blk = tm // CONV_HALO, t // CONV_HALO - 1
    return _call_with_casts(
        functools.partial(_inproj_kernel, chunk=chunk, conv_chunk=conv_chunk,
                          tiles_per_seq=seq // tm),
        casts,
        grid=(t // tm,),
        in_specs=[
            pl.BlockSpec((tm, D_MODEL), lambda i: (i, 0)),
            pl.BlockSpec((CONV_HALO, D_MODEL), lambda i: (jnp.maximum(i * per_tile - 1, 0), 0)),
            pl.BlockSpec((CONV_HALO, D_MODEL),
                         lambda i: (jnp.minimum((i + 1) * per_tile, last_blk), 0)),
            pl.BlockSpec((None, 1, D_MODEL), lambda i: (layer, 0, 0)),
            pl.BlockSpec((None, 3, D_MODEL), lambda i: (layer, 0, 0)),
            _resident((D_MODEL, IN_TOTAL), lambda i: (0, 0)),
        ],
        out_spec=pl.BlockSpec((tm, PROJ_W), lambda i: (i, 0)),
        out_shape=jax.ShapeDtypeStruct((t, PROJ_W), BF16),
        operands=(x, x, x, g, conv_w, w),
        compiler_params=_params(52, ("arbitrary",)),
        name="inproj",
    )


def _t5_bucket(rel):
    half = N_BUCKETS // 2
    max_exact = half // 2
    ret = jnp.where(rel > 0, half, 0)
    n = jnp.abs(rel)
    nf = jnp.maximum(n, 1).astype(jnp.float32)
    large = max_exact + (jnp.log(nf / max_exact) / math.log(MAX_DISTANCE / max_exact)
                         * (half - max_exact)).astype(jnp.int32)
    large = jnp.minimum(large, half - 1)
    return ret + jnp.where(n < max_exact, n, large)


def _bucket_map():
    kj = jnp.arange(3 * BLOCK)[:, None]
    qi = jnp.arange(BLOCK)[None, :]
    rel = kj - BLOCK - qi
    return jnp.where(jnp.abs(rel) <= WINDOW, _t5_bucket(rel), -1).astype(jnp.int32)


def _bias_kernel(relb_ref, bk_ref, o_ref):
    h = pl.program_id(0)
    bk = bk_ref[...]
    acc = jnp.full(bk.shape, NEG_INF, F32)
    for b in range(N_BUCKETS):
        acc = jnp.where(bk == b, relb_ref[b, h], acc)
    key = lax.broadcasted_iota(jnp.int32, bk.shape, 0)
    o_ref[VAR_MID] = acc
    o_ref[VAR_FIRST] = jnp.where(key < BLOCK, NEG_INF, acc)
    o_ref[VAR_LAST] = jnp.where(key >= 2 * BLOCK, NEG_INF, acc)


def _bias_table(rel_bias):
    return pl.pallas_call(
        _bias_kernel,
        grid=(N_HEADS,),
        in_specs=[
            pl.BlockSpec(memory_space=pltpu.SMEM),
            pl.BlockSpec((3 * BLOCK, BLOCK), lambda h: (0, 0)),
        ],
        out_specs=pl.BlockSpec((3, None, 3 * BLOCK, BLOCK), lambda h: (0, h, 0, 0)),
        out_shape=jax.ShapeDtypeStruct((3, N_HEADS, 3 * BLOCK, BLOCK), F32),
        compiler_params=_params(16, ("arbitrary",)),
        name="bias_table",
    )(rel_bias, _bucket_map())


def _attn_kernel(sink_ref, q_ref, kc_ref, kp_ref, kn_ref, vc_ref, vp_ref, vn_ref, bias_ref,
                 o_ref, klo_ref, khi_ref, vt_ref, *, layer, tq, blocks_per_seq):
    i = pl.program_id(0)
    nqb = tq // BLOCK
    assert 2 * HEAD_DIM == LANES and blocks_per_seq >= 2

    for pair in range(N_KV_HEADS // 2):
        r = 0
        for part in (kp_ref, kc_ref, kn_ref):
            n = part.shape[0]
            t = pltpu.bitcast(part[:, pair * LANES:(pair + 1) * LANES], jnp.uint32)
            moved = pltpu.roll(t, HEAD_DIM, 1)
            low = lax.broadcasted_iota(jnp.int32, t.shape, 1) < HEAD_DIM
            zero = jnp.zeros_like(t)
            for a, lo, hi in ((2 * pair, jnp.where(low, t, zero), jnp.where(low, zero, moved)),
                              (2 * pair + 1, jnp.where(low, moved, zero),
                               jnp.where(low, zero, t))):
                klo_ref[a, r:r + n, :] = pltpu.bitcast(lo, BF16)
                khi_ref[a, r:r + n, :] = pltpu.bitcast(hi, BF16)
            r += n
        c = 0
        for part in (vp_ref, vc_ref, vn_ref):
            for b0 in range(0, part.shape[0], BLOCK):
                blk = part[b0:b0 + BLOCK, pair * LANES:(pair + 1) * LANES].astype(F32)
                blk_t = blk.T.astype(BF16)
                vt_ref[2 * pair, 0:HEAD_DIM, c:c + BLOCK] = blk_t[0:HEAD_DIM, :]
                vt_ref[2 * pair + 1, 0:HEAD_DIM, c:c + BLOCK] = blk_t[HEAD_DIM:, :]
                c += BLOCK
    for a in range(N_KV_HEADS):
        vt_ref[a, HEAD_DIM:, :] = jnp.ones((SUM_ROWS, vt_ref.shape[2]), BF16)

    nt_dims = (((1,), (1,)), ((), ()))
    q_scale = jnp.asarray(HEAD_DIM ** -0.5, BF16)

    def scores(qb, a):
        q0 = qb * BLOCK
        c0 = a * GROUP * HEAD_DIM
        qt = jnp.concatenate([q_ref[q0:q0 + BLOCK, c0:c0 + LANES],
                              q_ref[q0:q0 + BLOCK, c0 + LANES:c0 + 2 * LANES]], axis=0)
        qt = qt * q_scale
        win = slice(q0, q0 + 3 * BLOCK)
        return [lax.dot_general(k_ref[a, win, :], qt, nt_dims, preferred_element_type=F32)
                for k_ref in (klo_ref, khi_ref)]

    def softmax(qb, a, s_par):
        n = lax.rem(i * nqb + qb, blocks_per_seq)
        var = jnp.where(n == 0, VAR_FIRST, jnp.where(n == blocks_per_seq - 1, VAR_LAST, VAR_MID))
        pts, sinks = [], []
        for par in range(2):
            p_t, e_t = [], []
            for tile in range(2):
                h = a * GROUP + 2 * tile + par
                sink = sink_ref[layer, h]
                sg = s_par[par][:, tile * BLOCK:(tile + 1) * BLOCK] + bias_ref[var, h]
                m = jnp.maximum(jnp.max(sg, axis=0, keepdims=True), sink)
                p_t.append(jnp.exp(sg - m).astype(BF16))
                e_t.append(jnp.exp(sink - m))
            pts.append(jnp.concatenate(p_t, axis=1))
            sinks.append(jnp.concatenate(e_t, axis=1))
        return pts, sinks

    def values(qb, a, pts, sinks):
        q0 = qb * BLOCK
        c0 = a * GROUP * HEAD_DIM
        v_t = vt_ref[a, :, q0:q0 + 3 * BLOCK]
        halves = []
        for par in range(2):
            o_full = jnp.dot(v_t, pts[par], preferred_element_type=F32)
            denom = o_full[HEAD_DIM:HEAD_DIM + 1, :] + sinks[par]
            halves.append(o_full[0:HEAD_DIM, :] / denom)
        o_t = jnp.concatenate(halves, axis=0)
        for tile in range(2):
            o_ref[q0:q0 + BLOCK, c0 + tile * LANES:c0 + (tile + 1) * LANES] = (
                o_t[:, tile * BLOCK:(tile + 1) * BLOCK].T.astype(BF16))

    units = [(qb, a) for qb in range(nqb) for a in range(N_KV_HEADS)]
    s_next = scores(*units[0])
    sm_prev = None
    for t in range(len(units) + 1):
        s_cur = s_next
        if t + 1 < len(units):
            s_next = scores(*units[t + 1])
        sm_cur = softmax(*units[t], s_cur) if t < len(units) else None
        if t >= 1:
            values(*units[t - 1], *sm_prev)
        sm_prev = sm_cur


def _attention(proj, bias, sink, layer, casts, *, seq, tq=ROW_TILE):
    t = proj.shape[0]
    assert seq % tq == 0 and tq % BLOCK == 0
    nb = tq // BLOCK
    last_blk = t // BLOCK - 1
    kvw = N_KV_HEADS * HEAD_DIM
    kcol, vcol = COL_K // kvw, COL_V // kvw
    prev_map = lambda c: (lambda i: (jnp.maximum(i * nb - 1, 0), c))
    next_map = lambda c: (lambda i: (jnp.minimum((i + 1) * nb, last_blk), c))
    k_scratch = pltpu.VMEM((N_KV_HEADS, tq + 2 * BLOCK, LANES), BF16)
    vt_scratch = pltpu.VMEM((N_KV_HEADS, HEAD_DIM + SUM_ROWS, tq + 2 * BLOCK), BF16)
    return _call_with_casts(
        functools.partial(_attn_kernel, layer=layer, tq=tq, blocks_per_seq=seq // BLOCK),
        casts,
        grid=(t // tq,),
        in_specs=[
            pl.BlockSpec(memory_space=pltpu.SMEM),
            pl.BlockSpec((tq, D_MODEL), lambda i: (i, COL_Q // D_MODEL)),
            pl.BlockSpec((tq, kvw), lambda i: (i, kcol)),
            pl.BlockSpec((BLOCK, kvw), prev_map(kcol)),
            pl.BlockSpec((BLOCK, kvw), next_map(kcol)),
            pl.BlockSpec((tq, kvw), lambda i: (i, vcol)),
            pl.BlockSpec((BLOCK, kvw), prev_map(vcol)),
            pl.BlockSpec((BLOCK, kvw), next_map(vcol)),
            _resident((3, N_HEADS, 3 * BLOCK, BLOCK), lambda i: (0, 0, 0, 0)),
        ],
        out_spec=pl.BlockSpec((tq, D_MODEL), lambda i: (i, 0)),
        out_shape=jax.ShapeDtypeStruct((t, D_MODEL), BF16),
        operands=(sink, proj, proj, proj, proj, proj, proj, proj, bias),
        scratch_shapes=[k_scratch, k_scratch, vt_scratch],
        compiler_params=_params(48, ("arbitrary",)),
        name="window_attn",
    )


def _pool_band(tm):
    t_i = jnp.arange(tm)[:, None]
    j_i = jnp.arange(tm + 2 * POOL_HALO)[None, :]
    d = j_i - POOL_HALO - t_i
    bands = [(d >= -(w // 2)) & (d <= w - 1 - w // 2) for w in POOL_WINDOWS]
    return jnp.stack(bands).astype(BF16)


def _mixer_merge_kernel(a_ref, u_ref, up_ref, un_ref, t_ref,
                        ga0_ref, ga1_ref, gp0_ref, gp1_ref, gt0_ref, gt1_ref, x_ref,
                        band_ref, wp_ref, ps_ref, wa_ref, wt_ref, wo_ref,
                        o_ref, *, tiles_per_seq):
    tm = a_ref.shape[0]
    seq = tm * tiles_per_seq
    n = lax.rem(pl.program_id(0), tiles_per_seq)
    first, last = n == 0, n == tiles_per_seq - 1
    cg = POOL_CG
    assert D_MODEL // cg == POOL_GROUPS

    def gate(lo_ref, hi_ref):
        g = jnp.concatenate([lo_ref[...], hi_ref[...]], axis=1)
        return jax.nn.sigmoid(g.astype(F32))

    def window_sum(g):
        cols = slice(g * cg, (g + 1) * cg)
        prev = jnp.where(first, 0.0, up_ref[:, cols].astype(F32)).astype(BF16)
        nxt = jnp.where(last, 0.0, un_ref[:, cols].astype(F32)).astype(BF16)
        ctx = jnp.concatenate([prev, u_ref[:, cols], nxt], axis=0)
        return jnp.dot(band_ref[g], ctx, preferred_element_type=F32)

    t_abs = n * tm + lax.broadcasted_iota(jnp.int32, (tm, 1), 0)

    def pooled(g, s):
        cols = slice(g * cg, (g + 1) * cg)
        win = POOL_WINDOWS[g]
        lo_off, hi_off = win // 2, win - 1 - win // 2
        cnt = jnp.minimum(t_abs + hi_off, seq - 1) - jnp.maximum(t_abs - lo_off, 0) + 1
        p = (s / cnt.astype(F32) - u_ref[:, cols].astype(F32)).astype(BF16)
        return jnp.dot(p, wp_ref[g], preferred_element_type=F32) * ps_ref[:, cols]

    sums = [window_sum(g) for g in range(POOL_GROUPS)]
    ya = jnp.dot(a_ref[...], wa_ref[...], preferred_element_type=F32)
    yp = jnp.concatenate([pooled(g, sums[g]) for g in range(POOL_GROUPS)], axis=1)
    merged = gate(ga0_ref, ga1_ref) * ya + gate(gp0_ref, gp1_ref) * yp
    yt = jnp.dot(t_ref[...], wt_ref[...], preferred_element_type=F32)
    merged = merged + gate(gt0_ref, gt1_ref) * yt
    o_ref[...] = x_ref[...] + jnp.dot(merged.astype(BF16), wo_ref[...],
                                      preferred_element_type=F32)


def _mixer_merge(proj, attn, x, w_pool, pool_scale, w_a_out, w_attn_out, w_o, layer,
                 *, seq, tm=ROW_TILE):
    t = x.shape[0]
    assert seq % tm == 0 and tm % POOL_HALO == 0

    def row(col_block):
        return pl.BlockSpec((tm, D_MODEL), lambda i: (i, col_block))

    def halo(rows, col_block):
        per_tile, last_blk = tm // rows, t // rows - 1
        return (pl.BlockSpec((rows, D_MODEL),
                             lambda i: (jnp.maximum(i * per_tile - 1, 0), col_block)),
                pl.BlockSpec((rows, D_MODEL),
                             lambda i: (jnp.minimum((i + 1) * per_tile, last_blk), col_block)))

    def gate(col):
        def spec(col_block):
            return pl.BlockSpec((tm, GATE_W), lambda i: (i, col_block))
        return [spec(col // GATE_W + half) for half in range(D_MODEL // GATE_W)]

    wspec = _resident((D_MODEL, D_MODEL), lambda i: (0, 0))
    band = _pool_band(tm)
    u_blk = COL_U // D_MODEL
    return pl.pallas_call(
        functools.partial(_mixer_merge_kernel, tiles_per_seq=seq // tm),
        grid=(t // tm,),
        in_specs=[row(COL_A // D_MODEL),
                  row(u_blk), *halo(POOL_HALO, u_blk),
                  row(0),
                  *gate(COL_GA), *gate(COL_GP), *gate(COL_GT),
                  row(0),
                  _resident(band.shape, lambda i: (0, 0, 0)),
                  _resident((POOL_GROUPS, POOL_CG, POOL_CG), lambda i: (0, 0, 0)),
                  pl.BlockSpec((None, 1, D_MODEL), lambda i: (layer, 0, 0)),
                  wspec, wspec, wspec],
        out_specs=row(0),
        out_shape=jax.ShapeDtypeStruct((t, D_MODEL), F32),
        compiler_params=_params(56, ("parallel",)),
        name="mixer_merge",
    )(proj, proj, proj, proj, attn,
      proj, proj, proj, proj, proj, proj, x,
      band, w_pool, pool_scale, w_a_out, w_attn_out, w_o)


def _ffn_kernel(x_ref, g_ref, wgu_ref, wd_ref, gf_ref, o_ref, *, chunk, final):
    x = x_ref[...]
    h = _rms(x, g_ref[...]).astype(BF16)
    acc = x
    for c0, n in _col_chunks(D_FF, chunk):
        gate = jnp.dot(h, wgu_ref[:, c0:c0 + n], preferred_element_type=F32)
        up = jnp.dot(h, wgu_ref[:, D_FF + c0:D_FF + c0 + n], preferred_element_type=F32)
        act = (jax.nn.silu(gate) * up).astype(BF16)
        acc = acc + jnp.dot(act, wd_ref[c0:c0 + n, :], preferred_element_type=F32)
    o_ref[...] = _rms(acc, gf_ref[...]) if final else acc


def _ffn(x, g_ffn, w_gu, w_down, g_final, layer, casts, *, final, tm=ROW_TILE, chunk=1536):
    t = x.shape[0]
    row = pl.BlockSpec((tm, D_MODEL), lambda i: (i, 0))
    return _call_with_casts(
        functools.partial(_ffn_kernel, chunk=chunk, final=final),
        casts,
        grid=(t // tm,),
        in_specs=[row,
                  pl.BlockSpec((None, 1, D_MODEL), lambda i: (layer, 0, 0)),
                  _resident((D_MODEL, 2 * D_FF), lambda i: (0, 0)),
                  _resident((D_FF, D_MODEL), lambda i: (0, 0)),
                  pl.BlockSpec((1, D_MODEL), lambda i: (0, 0))],
        out_spec=row,
        out_shape=jax.ShapeDtypeStruct((t, D_MODEL), F32),
        operands=(x, g_ffn, w_gu, w_down, g_final),
        compiler_params=_params(56, ("arbitrary",)),
        name="ffn",
    )


@jax.jit
def _trunk(x, w_in, conv_w, w_a_out, w_pool, pool_scale, w_attn_out, attn_sink, w_o,
           g_mix, g_ffn, w_gu, w_down, rel_bias, g_final):
    batch, seq, d = x.shape
    depth = w_in.shape[0]
    xf = x.reshape(batch * seq, d)

    conv_w3 = conv_w.reshape(depth, 3, d)
    pool_scale3 = pool_scale.reshape(depth, 1, d)
    g_mix3 = g_mix.reshape(depth, 1, d)
    g_ffn3 = g_ffn.reshape(depth, 1, d)
    g_final2 = g_final.reshape(1, d)
    w_pool2 = w_pool.reshape(depth, POOL_GROUPS * POOL_CG, POOL_CG)
    bias = _bias_table(rel_bias)

    steps = (batch * seq) // ROW_TILE
    w_in_l = w_in[0].astype(BF16)
    for l in range(depth):
        proj, wa_b, wt_b, wo_b, wp_b = _inproj(
            xf, g_mix3, conv_w3, w_in_l, l,
            [_Cast(w, l, d // steps) for w in (w_a_out, w_attn_out, w_o, w_pool2)], seq=seq)
        attn, wgu_b, wd_b = _attention(
            proj, bias, attn_sink, l,
            [_Cast(w_gu, l, d // steps), _Cast(w_down, l, BLOCK)], seq=seq)
        x1 = _mixer_merge(proj, attn, xf, wp_b.reshape(POOL_GROUPS, POOL_CG, POOL_CG),
                          pool_scale3, wa_b, wt_b, wo_b, l, seq=seq)
        last = l == depth - 1
        xf, *nxt = _ffn(x1, g_ffn3, wgu_b, wd_b, g_final2, l,
                        [] if last else [_Cast(w_in, l + 1, d // steps)], final=last)
        w_in_l = nxt[0] if nxt else None
    return xf.reshape(batch, seq, d)


def kernel(x, w_in, conv_w, w_a_out, w_pool, pool_scale, w_attn_out, attn_sink, w_o, g_mix,
           g_ffn, w_gu, w_down, rel_bias, g_final):
    return _trunk(x, w_in, conv_w, w_a_out, w_pool, pool_scale, w_attn_out, attn_sink, w_o,
                  g_mix, g_ffn, w_gu, w_down, rel_bias, g_final)
```

```python
import functools
import math

import jax
import jax.numpy as jnp
from jax import lax
from jax.experimental import pallas as pl
from jax.experimental.pallas import tpu as pltpu

F32 = jnp.float32
BF16 = jnp.bfloat16

D_MODEL = 1024
N_HEADS = 16
N_KV_HEADS = 4
HEAD_DIM = 64
GROUP = N_HEADS // N_KV_HEADS
WINDOW = 128
BLOCK = 128
N_BUCKETS = 32
MAX_DISTANCE = 128
POOL_GROUPS = 4
POOL_CG = D_MODEL // POOL_GROUPS
D_FF = 2816
EPS = 1e-6
NEG_INF = -1e30

W_COL_B, W_COL_C, W_COL_X, W_COL_REST = 0, 1024, 2048, 3072
IN_TOTAL = 8704
COL_A, COL_U, COL_Q = 0, 1024, 2048
COL_K, COL_V = 3072, 3328
COL_GA, COL_GP, COL_GT = 3584, 4608, 5632
PROJ_W = D_MODEL + IN_TOTAL - W_COL_REST
GATE_W = 512

LANES = 128
MXU_W = 256
ROW_TILE = 512
ATTN_TILE = 1024
W_DOWN_CAST_BLOCKS = 11
BF16_ROWS = 16
SUM_ROWS = BF16_ROWS
CONV_HALO = BF16_ROWS
POOL_HALO = 64
POOL_WINDOWS = (2, 4, 8, 16)
MIB = 1024 * 1024

VAR_MID, VAR_FIRST, VAR_LAST = 0, 1, 2


def _params(vmem_mib, sem):
    return pltpu.CompilerParams(dimension_semantics=sem, vmem_limit_bytes=vmem_mib * MIB)


def _resident(shape, index_map):
    return pl.BlockSpec(shape, index_map, pipeline_mode=pl.Buffered(1))


def _rms(x, g):
    ms = jnp.mean(x * x, axis=-1, keepdims=True)
    return (x * lax.rsqrt(ms + EPS)) * g


class _Cast:
    def __init__(self, weights, layer, rows):
        _, r, c = weights.shape
        assert r % rows == 0 and rows % BF16_ROWS == 0
        self.weights, self.layer, self.rows, self.r, self.c = weights, layer, rows, r, c

    def specs(self, n_steps):
        last = self.r // self.rows - 1
        assert last < n_steps
        layer = self.layer
        return (pl.BlockSpec((None, self.rows, self.c),
                             lambda i: (layer, jnp.minimum(i, last), 0)),
                pl.BlockSpec((self.rows, self.c), lambda i: (jnp.minimum(i, last), 0)),
                jax.ShapeDtypeStruct((self.r, self.c), BF16))


def _with_casts(body, n_in, n_casts):
    def kernel(*refs):
        own_in = refs[:n_in]
        cast_src = refs[n_in:n_in + n_casts]
        out = refs[n_in + n_casts]
        cast_dst = refs[n_in + n_casts + 1:n_in + 2 * n_casts + 1]
        scratch = refs[n_in + 2 * n_casts + 1:]
        for src, dst in zip(cast_src, cast_dst):
            dst[...] = src[...].astype(BF16)
        body(*own_in, out, *scratch)
    return kernel


def _call_with_casts(body, casts, *, grid, in_specs, out_spec, out_shape, operands, **kwargs):
    n_steps = grid[0]
    cast_specs = [c.specs(n_steps) for c in casts]
    outs = pl.pallas_call(
        _with_casts(body, len(in_specs), len(casts)),
        grid=grid,
        in_specs=list(in_specs) + [s[0] for s in cast_specs],
        out_specs=[out_spec] + [s[1] for s in cast_specs],
        out_shape=[out_shape] + [s[2] for s in cast_specs],
        **kwargs,
    )(*operands, *[c.weights for c in casts])
    return tuple(outs)


def _col_chunks(total, width):
    assert total % MXU_W == 0 and width % MXU_W == 0
    return [(c, min(width, total - c)) for c in range(0, total, width)]


def _shift(x, k):
    n = x.shape[0]
    return pltpu.roll(x, k % n, 0)


def _inproj_kernel(x_ref, xp_ref, xn_ref, g_ref, cw_ref, w_ref, o_ref,
                   *, chunk, conv_chunk, tiles_per_seq):
    tm = x_ref.shape[0]
    halo = xp_ref.shape[0]
    n = lax.rem(pl.program_id(0), tiles_per_seq)
    g = g_ref[...]
    h = _rms(x_ref[...], g).astype(BF16)
    h_prev = jnp.where(n == 0, 0.0, _rms(xp_ref[...], g)).astype(BF16)
    h_next = jnp.where(n == tiles_per_seq - 1, 0.0, _rms(xn_ref[...], g)).astype(BF16)
    h_ext = jnp.concatenate([h_prev, h, h_next], axis=0)

    for c0, nc in _col_chunks(D_MODEL, conv_chunk):
        def proj(lhs, col):
            return jnp.dot(lhs, w_ref[:, col + c0:col + c0 + nc], preferred_element_type=F32)

        u = proj(h_ext, W_COL_C) * proj(h_ext, W_COL_X)
        w = cw_ref[:, c0:c0 + nc]
        y = _shift(u, 1) * w[0:1, :] + u * w[1:2, :] + _shift(u, -1) * w[2:3, :]
        o_ref[:, COL_A + c0:COL_A + c0 + nc] = (
            proj(h, W_COL_B) * y[halo:halo + tm, :]).astype(BF16)

    for c0, nc in _col_chunks(IN_TOTAL - W_COL_REST, chunk):
        o_ref[:, COL_U + c0:COL_U + c0 + nc] = jnp.dot(
            h, w_ref[:, W_COL_REST + c0:W_COL_REST + c0 + nc],
            preferred_element_type=F32).astype(BF16)


def _inproj(x, g, conv_w, w, layer, casts, *, seq, tm=ROW_TILE, chunk=1024, conv_chunk=256):
    t = x.shape[0]
    assert seq % tm == 0 and tm % CONV_HALO == 0
    per_tile, last_blk = tm // CONV_HALO, t // CONV_HALO - 1
    return _call_with_casts(
        functools.partial(_inproj_kernel, chunk=chunk, conv_chunk=conv_chunk,
                          tiles_per_seq=seq // tm),
        casts,
        grid=(t // tm,),
        in_specs=[
            pl.BlockSpec((tm, D_MODEL), lambda i: (i, 0)),
            pl.BlockSpec((CONV_HALO, D_MODEL), lambda i: (jnp.maximum(i * per_tile - 1, 0), 0)),
            pl.BlockSpec((CONV_HALO, D_MODEL),
                         lambda i: (jnp.minimum((i + 1) * per_tile, last_blk), 0)),
            pl.BlockSpec((None, 1, D_MODEL), lambda i: (layer, 0, 0)),
            pl.BlockSpec((None, 3, D_MODEL), lambda i: (layer, 0, 0)),
            _resident((D_MODEL, IN_TOTAL), lambda i: (0, 0)),
        ],
        out_spec=pl.BlockSpec((tm, PROJ_W), lambda i: (i, 0)),
        out_shape=jax.ShapeDtypeStruct((t, PROJ_W), BF16),
        operands=(x, x, x, g, conv_w, w),
        compiler_params=_params(52, ("arbitrary",)),
        name="inproj",
    )


def _t5_bucket(rel):
    half = N_BUCKETS // 2
    max_exact = half // 2
    ret = jnp.where(rel > 0, half, 0)
    n = jnp.abs(rel)
    nf = jnp.maximum(n, 1).astype(jnp.float32)
    large = max_exact + (jnp.log(nf / max_exact) / math.log(MAX_DISTANCE / max_exact)
                         * (half - max_exact)).astype(jnp.int32)
    large = jnp.minimum(large, half - 1)
    return ret + jnp.where(n < max_exact, n, large)


def _bucket_map():
    kj = jnp.arange(3 * BLOCK)[:, None]
    qi = jnp.arange(BLOCK)[None, :]
    rel = kj - BLOCK - qi
    return jnp.where(jnp.abs(rel) <= WINDOW, _t5_bucket(rel), -1).astype(jnp.int32)


def _bias_kernel(relb_ref, bk_ref, o_ref):
    h = pl.program_id(0)
    bk = bk_ref[...]
    acc = jnp.full(bk.shape, NEG_INF, F32)
    for b in range(N_BUCKETS):
        acc = jnp.where(bk == b, relb_ref[b, h], acc)
    key = lax.broadcasted_iota(jnp.int32, bk.shape, 0)
    o_ref[VAR_MID] = acc
    o_ref[VAR_FIRST] = jnp.where(key < BLOCK, NEG_INF, acc)
    o_ref[VAR_LAST] = jnp.where(key >= 2 * BLOCK, NEG_INF, acc)


def _bias_table(rel_bias, casts):
    return _call_with_casts(
        _bias_kernel,
        casts,
        grid=(N_HEADS,),
        in_specs=[
            pl.BlockSpec(memory_space=pltpu.SMEM),
            pl.BlockSpec((3 * BLOCK, BLOCK), lambda h: (0, 0)),
        ],
        out_spec=pl.BlockSpec((3, None, 3 * BLOCK, BLOCK), lambda h: (0, h, 0, 0)),
        out_shape=jax.ShapeDtypeStruct((3, N_HEADS, 3 * BLOCK, BLOCK), F32),
        operands=(rel_bias, _bucket_map()),
        compiler_params=_params(16, ("arbitrary",)),
        name="bias_table",
    )


def _attn_kernel(sink_ref, q_ref, kc_ref, kp_ref, kn_ref, vc_ref, vp_ref, vn_ref, bias_ref,
                 o_ref, klo_ref, khi_ref, vt_ref, *, layer, tq, blocks_per_seq):
    i = pl.program_id(0)
    nqb = tq // BLOCK
    assert 2 * HEAD_DIM == LANES and blocks_per_seq >= 2

    for pair in range(N_KV_HEADS // 2):
        r = 0
        for part in (kp_ref, kc_ref, kn_ref):
            n = part.shape[0]
            t = pltpu.bitcast(part[:, pair * LANES:(pair + 1) * LANES], jnp.uint32)
            moved = pltpu.roll(t, HEAD_DIM, 1)
            low = lax.broadcasted_iota(jnp.int32, t.shape, 1) < HEAD_DIM
            zero = jnp.zeros_like(t)
            for a, lo, hi in ((2 * pair, jnp.where(low, t, zero), jnp.where(low, zero, moved)),
                              (2 * pair + 1, jnp.where(low, moved, zero),
                               jnp.where(low, zero, t))):
                klo_ref[a, r:r + n, :] = pltpu.bitcast(lo, BF16)
                khi_ref[a, r:r + n, :] = pltpu.bitcast(hi, BF16)
            r += n
        c = 0
        for part in (vp_ref, vc_ref, vn_ref):
            for b0 in range(0, part.shape[0], BLOCK):
                blk = part[b0:b0 + BLOCK, pair * LANES:(pair + 1) * LANES].astype(F32)
                blk_t = blk.T.astype(BF16)
                vt_ref[2 * pair, 0:HEAD_DIM, c:c + BLOCK] = blk_t[0:HEAD_DIM, :]
                vt_ref[2 * pair + 1, 0:HEAD_DIM, c:c + BLOCK] = blk_t[HEAD_DIM:, :]
                c += BLOCK
    for a in range(N_KV_HEADS):
        vt_ref[a, HEAD_DIM:, :] = jnp.ones((SUM_ROWS, vt_ref.shape[2]), BF16)

    nt_dims = (((1,), (1,)), ((), ()))
    q_scale = jnp.asarray(HEAD_DIM ** -0.5, BF16)

    def scores(qb, a):
        q0 = qb * BLOCK
        c0 = a * GROUP * HEAD_DIM
        qt = jnp.concatenate([q_ref[q0:q0 + BLOCK, c0:c0 + LANES],
                              q_ref[q0:q0 + BLOCK, c0 + LANES:c0 + 2 * LANES]], axis=0)
        qt = qt * q_scale
        win = slice(q0, q0 + 3 * BLOCK)
        return [lax.dot_general(k_ref[a, win, :], qt, nt_dims, preferred_element_type=F32)
                for k_ref in (klo_ref, khi_ref)]

    def softmax(qb, a, s_par):
        n = lax.rem(i * nqb + qb, blocks_per_seq)
        var = jnp.where(n == 0, VAR_FIRST, jnp.where(n == blocks_per_seq - 1, VAR_LAST, VAR_MID))
        pts, sinks = [], []
        for par in range(2):
            p_t, e_t = [], []
            for tile in range(2):
                h = a * GROUP + 2 * tile + par
                sink = sink_ref[layer, h]
                sg = s_par[par][:, tile * BLOCK:(tile + 1) * BLOCK] + bias_ref[var, h]
                m = jnp.maximum(jnp.max(sg, axis=0, keepdims=True), sink)
                p_t.append(jnp.exp(sg - m).astype(BF16))
                e_t.append(jnp.exp(sink - m))
            pts.append(jnp.concatenate(p_t, axis=1))
            sinks.append(jnp.concatenate(e_t, axis=1))
        return pts, sinks

    def values(qb, a, pts, sinks):
        q0 = qb * BLOCK
        c0 = a * GROUP * HEAD_DIM
        v_t = vt_ref[a, :, q0:q0 + 3 * BLOCK]
        halves = []
        for par in range(2):
            o_full = jnp.dot(v_t, pts[par], preferred_element_type=F32)
            denom = o_full[HEAD_DIM:HEAD_DIM + 1, :] + sinks[par]
            halves.append(o_full[0:HEAD_DIM, :] / denom)
        o_t = jnp.concatenate(halves, axis=0)
        for tile in range(2):
            o_ref[q0:q0 + BLOCK, c0 + tile * LANES:c0 + (tile + 1) * LANES] = (
                o_t[:, tile * BLOCK:(tile + 1) * BLOCK].T.astype(BF16))

    units = [(qb, a) for qb in range(nqb) for a in range(N_KV_HEADS)]
    s_next = scores(*units[0])
    sm_prev = None
    for t in range(len(units) + 1):
        s_cur = s_next
        if t + 1 < len(units):
            s_next = scores(*units[t + 1])
        sm_cur = softmax(*units[t], s_cur) if t < len(units) else None
        if t >= 1:
            values(*units[t - 1], *sm_prev)
        sm_prev = sm_cur


def _attention(proj, bias, sink, layer, casts, *, seq, tq=ATTN_TILE):
    t = proj.shape[0]
    assert seq % tq == 0 and tq % BLOCK == 0
    nb = tq // BLOCK
    last_blk = t // BLOCK - 1
    kvw = N_KV_HEADS * HEAD_DIM
    kcol, vcol = COL_K // kvw, COL_V // kvw
    prev_map = lambda c: (lambda i: (jnp.maximum(i * nb - 1, 0), c))
    next_map = lambda c: (lambda i: (jnp.minimum((i + 1) * nb, last_blk), c))
    k_scratch = pltpu.VMEM((N_KV_HEADS, tq + 2 * BLOCK, LANES), BF16)
    vt_scratch = pltpu.VMEM((N_KV_HEADS, HEAD_DIM + SUM_ROWS, tq + 2 * BLOCK), BF16)
    return _call_with_casts(
        functools.partial(_attn_kernel, layer=layer, tq=tq, blocks_per_seq=seq // BLOCK),
        casts,
        grid=(t // tq,),
        in_specs=[
            pl.BlockSpec(memory_space=pltpu.SMEM),
            pl.BlockSpec((tq, D_MODEL), lambda i: (i, COL_Q // D_MODEL)),
            pl.BlockSpec((tq, kvw), lambda i: (i, kcol)),
            pl.BlockSpec((BLOCK, kvw), prev_map(kcol)),
            pl.BlockSpec((BLOCK, kvw), next_map(kcol)),
            pl.BlockSpec((tq, kvw), lambda i: (i, vcol)),
            pl.BlockSpec((BLOCK, kvw), prev_map(vcol)),
            pl.BlockSpec((BLOCK, kvw), next_map(vcol)),
            _resident((3, N_HEADS, 3 * BLOCK, BLOCK), lambda i: (0, 0, 0, 0)),
        ],
        out_spec=pl.BlockSpec((tq, D_MODEL), lambda i: (i, 0)),
        out_shape=jax.ShapeDtypeStruct((t, D_MODEL), BF16),
        operands=(sink, proj, proj, proj, proj, proj, proj, proj, bias),
        scratch_shapes=[k_scratch, k_scratch, vt_scratch],
        compiler_params=_params(48, ("arbitrary",)),
        name="window_attn",
    )


def _pool_band(tm):
    t_i = jnp.arange(tm)[:, None]
    j_i = jnp.arange(tm + 2 * POOL_HALO)[None, :]
    d = j_i - POOL_HALO - t_i
    bands = [(d >= -(w // 2)) & (d <= w - 1 - w // 2) for w in POOL_WINDOWS]
    return jnp.stack(bands).astype(BF16)


def _mixer_merge_kernel(a_ref, u_ref, up_ref, un_ref, t_ref,
                        ga0_ref, ga1_ref, gp0_ref, gp1_ref, gt0_ref, gt1_ref, x_ref,
                        band_ref, wp_ref, ps_ref, wa_ref, wt_ref, wo_ref,
                        o_ref, *, tiles_per_seq):
    tm = a_ref.shape[0]
    seq = tm * tiles_per_seq
    n = lax.rem(pl.program_id(0), tiles_per_seq)
    first, last = n == 0, n == tiles_per_seq - 1
    cg = POOL_CG
    assert D_MODEL // cg == POOL_GROUPS

    def gate(lo_ref, hi_ref):
        g = jnp.concatenate([lo_ref[...], hi_ref[...]], axis=1)
        return jax.nn.sigmoid(g.astype(F32))

    def window_sum(g):
        cols = slice(g * cg, (g + 1) * cg)
        prev = jnp.where(first, 0.0, up_ref[:, cols].astype(F32)).astype(BF16)
        nxt = jnp.where(last, 0.0, un_ref[:, cols].astype(F32)).astype(BF16)
        ctx = jnp.concatenate([prev, u_ref[:, cols], nxt], axis=0)
        return jnp.dot(band_ref[g], ctx, preferred_element_type=F32)

    t_abs = n * tm + lax.broadcasted_iota(jnp.int32, (tm, 1), 0)

    def pooled(g, s):
        cols = slice(g * cg, (g + 1) * cg)
        win = POOL_WINDOWS[g]
        lo_off, hi_off = win // 2, win - 1 - win // 2
        cnt = jnp.minimum(t_abs + hi_off, seq - 1) - jnp.maximum(t_abs - lo_off, 0) + 1
        p = (s / cnt.astype(F32) - u_ref[:, cols].astype(F32)).astype(BF16)
        return jnp.dot(p, wp_ref[g], preferred_element_type=F32) * ps_ref[:, cols]

    sums = [window_sum(g) for g in range(POOL_GROUPS)]
    ya = jnp.dot(a_ref[...], wa_ref[...], preferred_element_type=F32)
    yp = jnp.concatenate([pooled(g, sums[g]) for g in range(POOL_GROUPS)], axis=1)
    merged = gate(ga0_ref, ga1_ref) * ya + gate(gp0_ref, gp1_ref) * yp
    yt = jnp.dot(t_ref[...], wt_ref[...], preferred_element_type=F32)
    merged = merged + gate(gt0_ref, gt1_ref) * yt
    o_ref[...] = x_ref[...] + jnp.dot(merged.astype(BF16), wo_ref[...],
                                      preferred_element_type=F32)


def _mixer_merge(proj, attn, x, w_pool, pool_scale, w_a_out, w_attn_out, w_o, layer,
                 *, seq, tm=ROW_TILE):
    t = x.shape[0]
    assert seq % tm == 0 and tm % POOL_HALO == 0

    def row(col_block):
        return pl.BlockSpec((tm, D_MODEL), lambda i: (i, col_block))

    def halo(rows, col_block):
        per_tile, last_blk = tm // rows, t // rows - 1
        return (pl.BlockSpec((rows, D_MODEL),
                             lambda i: (jnp.maximum(i * per_tile - 1, 0), col_block)),
                pl.BlockSpec((rows, D_MODEL),
                             lambda i: (jnp.minimum((i + 1) * per_tile, last_blk), col_block)))

    def gate(col):
        def spec(col_block):
            return pl.BlockSpec((tm, GATE_W), lambda i: (i, col_block))
        return [spec(col // GATE_W + half) for half in range(D_MODEL // GATE_W)]

    wspec = _resident((D_MODEL, D_MODEL), lambda i: (0, 0))
    band = _pool_band(tm)
    u_blk = COL_U // D_MODEL
    return pl.pallas_call(
        functools.partial(_mixer_merge_kernel, tiles_per_seq=seq // tm),
        grid=(t // tm,),
        in_specs=[row(COL_A // D_MODEL),
                  row(u_blk), *halo(POOL_HALO, u_blk),
                  row(0),
                  *gate(COL_GA), *gate(COL_GP), *gate(COL_GT),
                  row(0),
                  _resident(band.shape, lambda i: (0, 0, 0)),
                  _resident((POOL_GROUPS, POOL_CG, POOL_CG), lambda i: (0, 0, 0)),
                  pl.BlockSpec((None, 1, D_MODEL), lambda i: (layer, 0, 0)),
                  wspec, wspec, wspec],
        out_specs=row(0),
        out_shape=jax.ShapeDtypeStruct((t, D_MODEL), F32),
        compiler_params=_params(56, ("parallel",)),
        name="mixer_merge",
    )(proj, proj, proj, proj, attn,
      proj, proj, proj, proj, proj, proj, x,
      band, w_pool, pool_scale, w_a_out, w_attn_out, w_o)


def _ffn_kernel(x_ref, g_ref, wgu_ref, wd_ref, gf_ref, o_ref, *, chunk, final):
    x = x_ref[...]
    h = _rms(x, g_ref[...]).astype(BF16)
    acc = x
    for c0, n in _col_chunks(D_FF, chunk):
        gate = jnp.dot(h, wgu_ref[:, c0:c0 + n], preferred_element_type=F32)
        up = jnp.dot(h, wgu_ref[:, D_FF + c0:D_FF + c0 + n], preferred_element_type=F32)
        act = (jax.nn.silu(gate) * up).astype(BF16)
        acc = acc + jnp.dot(act, wd_ref[c0:c0 + n, :], preferred_element_type=F32)
    o_ref[...] = _rms(acc, gf_ref[...]) if final else acc


def _ffn(x, g_ffn, w_gu, w_down, g_final, layer, casts, *, final, tm=ROW_TILE, chunk=1536):
    t = x.shape[0]
    row = pl.BlockSpec((tm, D_MODEL), lambda i: (i, 0))
    return _call_with_casts(
        functools.partial(_ffn_kernel, chunk=chunk, final=final),
        casts,
        grid=(t // tm,),
        in_specs=[row,
                  pl.BlockSpec((None, 1, D_MODEL), lambda i: (layer, 0, 0)),
                  _resident((D_MODEL, 2 * D_FF), lambda i: (0, 0)),
                  _resident((D_FF, D_MODEL), lambda i: (0, 0)),
                  pl.BlockSpec((1, D_MODEL), lambda i: (0, 0))],
        out_spec=row,
        out_shape=jax.ShapeDtypeStruct((t, D_MODEL), F32),
        operands=(x, g_ffn, w_gu, w_down, g_final),
        compiler_params=_params(56, ("arbitrary",)),
        name="ffn",
    )


@jax.jit
def _trunk(x, w_in, conv_w, w_a_out, w_pool, pool_scale, w_attn_out, attn_sink, w_o,
           g_mix, g_ffn, w_gu, w_down, rel_bias, g_final):
    batch, seq, d = x.shape
    depth = w_in.shape[0]
    xf = x.reshape(batch * seq, d)

    conv_w3 = conv_w.reshape(depth, 3, d)
    pool_scale3 = pool_scale.reshape(depth, 1, d)
    g_mix3 = g_mix.reshape(depth, 1, d)
    g_ffn3 = g_ffn.reshape(depth, 1, d)
    g_final2 = g_final.reshape(1, d)
    w_pool2 = w_pool.reshape(depth, POOL_GROUPS * POOL_CG, POOL_CG)
    steps = (batch * seq) // ROW_TILE
    attn_steps = (batch * seq) // ATTN_TILE
    bias, w_in_l = _bias_table(rel_bias, [_Cast(w_in, 0, d // N_HEADS)])
    for l in range(depth):
        proj, wa_b, wt_b, wo_b, wp_b = _inproj(
            xf, g_mix3, conv_w3, w_in_l, l,
            [_Cast(w, l, d // steps) for w in (w_a_out, w_attn_out, w_o, w_pool2)], seq=seq)
        attn, wgu_b, wd_b = _attention(
            proj, bias, attn_sink, l,
            [_Cast(w_gu, l, d // attn_steps), _Cast(w_down, l, D_FF // W_DOWN_CAST_BLOCKS)],
            seq=seq)
        x1 = _mixer_merge(proj, attn, xf, wp_b.reshape(POOL_GROUPS, POOL_CG, POOL_CG),
                          pool_scale3, wa_b, wt_b, wo_b, l, seq=seq)
        last = l == depth - 1
        xf, *nxt = _ffn(x1, g_ffn3, wgu_b, wd_b, g_final2, l,
                        [] if last else [_Cast(w_in, l + 1, d // steps)], final=last)
        w_in_l = nxt[0] if nxt else None
    return xf.reshape(batch, seq, d)


def kernel(x, w_in, conv_w, w_a_out, w_pool, pool_scale, w_attn_out, attn_sink, w_o, g_mix,
           g_ffn, w_gu, w_down, rel_bias, g_final):
    return _trunk(x, w_in, conv_w, w_a_out, w_pool, pool_scale, w_attn_out, attn_sink, w_o,
                  g_mix, g_ffn, w_gu, w_down, rel_bias, g_final)
```

```python
import functools
import math

import jax
import jax.numpy as jnp
from jax import lax
from jax.experimental import pallas as pl
from jax.experimental.pallas import tpu as pltpu

F32 = jnp.float32
BF16 = jnp.bfloat16

D_MODEL = 1024
N_HEADS = 16
N_KV_HEADS = 4
HEAD_DIM = 64
GROUP = N_HEADS // N_KV_HEADS
WINDOW = 128
BLOCK = 128
N_BUCKETS = 32
MAX_DISTANCE = 128
POOL_GROUPS = 4
POOL_CG = D_MODEL // POOL_GROUPS
D_FF = 2816
EPS = 1e-6
NEG_INF = -1e30

W_COL_B, W_COL_C, W_COL_X, W_COL_REST = 0, 1024, 2048, 3072
IN_TOTAL = 8704
COL_A, COL_U, COL_Q = 0, 1024, 2048
COL_K, COL_V = 3072, 3328
COL_GA, COL_GP, COL_GT = 3584, 4608, 5632
PROJ_W = D_MODEL + IN_TOTAL - W_COL_REST
GATE_W = 512

LANES = 128
MXU_W = 256
ROW_TILE = 512
ATTN_TILE = 1024
W_DOWN_CAST_BLOCKS = 11
BF16_ROWS = 16
SUM_ROWS = BF16_ROWS
CONV_HALO = BF16_ROWS
POOL_HALO = 64
POOL_SUB = 256
POOL_WINDOWS = (2, 4, 8, 16)
MIB = 1024 * 1024

VAR_MID, VAR_FIRST, VAR_LAST = 0, 1, 2


def _params(vmem_mib, sem):
    return pltpu.CompilerParams(dimension_semantics=sem, vmem_limit_bytes=vmem_mib * MIB)


def _resident(shape, index_map):
    return pl.BlockSpec(shape, index_map, pipeline_mode=pl.Buffered(1))


def _rms(x, g):
    ms = jnp.mean(x * x, axis=-1, keepdims=True)
    return (x * lax.rsqrt(ms + EPS)) * g


class _Cast:
    def __init__(self, weights, layer, rows):
        _, r, c = weights.shape
        assert r % rows == 0 and rows % BF16_ROWS == 0
        self.weights, self.layer, self.rows, self.r, self.c = weights, layer, rows, r, c

    def specs(self, n_steps):
        last = self.r // self.rows - 1
        assert last < n_steps
        layer = self.layer
        return (pl.BlockSpec((None, self.rows, self.c),
                             lambda i: (layer, jnp.minimum(i, last), 0)),
                pl.BlockSpec((self.rows, self.c), lambda i: (jnp.minimum(i, last), 0)),
                jax.ShapeDtypeStruct((self.r, self.c), BF16))


def _with_casts(body, n_in, n_casts):
    def kernel(*refs):
        own_in = refs[:n_in]
        cast_src = refs[n_in:n_in + n_casts]
        out = refs[n_in + n_casts]
        cast_dst = refs[n_in + n_casts + 1:n_in + 2 * n_casts + 1]
        scratch = refs[n_in + 2 * n_casts + 1:]
        for src, dst in zip(cast_src, cast_dst):
            dst[...] = src[...].astype(BF16)
        body(*own_in, out, *scratch)
    return kernel


def _call_with_casts(body, casts, *, grid, in_specs, out_spec, out_shape, operands, **kwargs):
    n_steps = grid[0]
    cast_specs = [c.specs(n_steps) for c in casts]
    outs = pl.pallas_call(
        _with_casts(body, len(in_specs), len(casts)),
        grid=grid,
        in_specs=list(in_specs) + [s[0] for s in cast_specs],
        out_specs=[out_spec] + [s[1] for s in cast_specs],
        out_shape=[out_shape] + [s[2] for s in cast_specs],
        **kwargs,
    )(*operands, *[c.weights for c in casts])
    return tuple(outs)


def _col_chunks(total, width):
    assert total % MXU_W == 0 and width % MXU_W == 0
    return [(c, min(width, total - c)) for c in range(0, total, width)]


def _shift(x, k):
    n = x.shape[0]
    return pltpu.roll(x, k % n, 0)


def _inproj_kernel(x_ref, xp_ref, xn_ref, g_ref, cw_ref, w_ref, o_ref,
                   *, chunk, conv_chunk, tiles_per_seq):
    tm = x_ref.shape[0]
    halo = xp_ref.shape[0]
    n = lax.rem(pl.program_id(0), tiles_per_seq)
    g = g_ref[...]
    h = _rms(x_ref[...], g).astype(BF16)
    h_prev = jnp.where(n == 0, 0.0, _rms(xp_ref[...], g)).astype(BF16)
    h_next = jnp.where(n == tiles_per_seq - 1, 0.0, _rms(xn_ref[...], g)).astype(BF16)
    h_ext = jnp.concatenate([h_prev, h, h_next], axis=0)

    for c0, nc in _col_chunks(D_MODEL, conv_chunk):
        def proj(lhs, col):
            return jnp.dot(lhs, w_ref[:, col + c0:col + c0 + nc], preferred_element_type=F32)

        u = proj(h_ext, W_COL_C) * proj(h_ext, W_COL_X)
        w = cw_ref[:, c0:c0 + nc]
        y = _shift(u, 1) * w[0:1, :] + u * w[1:2, :] + _shift(u, -1) * w[2:3, :]
        o_ref[:, COL_A + c0:COL_A + c0 + nc] = (
            proj(h, W_COL_B) * y[halo:halo + tm, :]).astype(BF16)

    for c0, nc in _col_chunks(IN_TOTAL - W_COL_REST, chunk):
        o_ref[:, COL_U + c0:COL_U + c0 + nc] = jnp.dot(
            h, w_ref[:, W_COL_REST + c0:W_COL_REST + c0 + nc],
            preferred_element_type=F32).astype(BF16)


def _inproj(x, g, conv_w, w, layer, casts, *, seq, tm=ROW_TILE, chunk=1024, conv_chunk=256):
    t = x.shape[0]
    assert seq % tm == 0 and tm % CONV_HALO == 0
    per_tile, last_blk = tm // CONV_HALO, t // CONV_HALO - 1
    return _call_with_casts(
        functools.partial(_inproj_kernel, chunk=chunk, conv_chunk=conv_chunk,
                          tiles_per_seq=seq // tm),
        casts,
        grid=(t // tm,),
        in_specs=[
            pl.BlockSpec((tm, D_MODEL), lambda i: (i, 0)),
            pl.BlockSpec((CONV_HALO, D_MODEL), lambda i: (jnp.maximum(i * per_tile - 1, 0), 0)),
            pl.BlockSpec((CONV_HALO, D_MODEL),
                         lambda i: (jnp.minimum((i + 1) * per_tile, last_blk), 0)),
            pl.BlockSpec((None, 1, D_MODEL), lambda i: (layer, 0, 0)),
            pl.BlockSpec((None, 3, D_MODEL), lambda i: (layer, 0, 0)),
            _resident((D_MODEL, IN_TOTAL), lambda i: (0, 0)),
        ],
        out_spec=pl.BlockSpec((tm, PROJ_W), lambda i: (i, 0)),
        out_shape=jax.ShapeDtypeStruct((t, PROJ_W), BF16),
        operands=(x, x, x, g, conv_w, w),
        compiler_params=_params(52, ("arbitrary",)),
        name="inproj",
    )


def _t5_bucket(rel):
    half = N_BUCKETS // 2
    max_exact = half // 2
    ret = jnp.where(rel > 0, half, 0)
    n = jnp.abs(rel)
    nf = jnp.maximum(n, 1).astype(jnp.float32)
    large = max_exact + (jnp.log(nf / max_exact) / math.log(MAX_DISTANCE / max_exact)
                         * (half - max_exact)).astype(jnp.int32)
    large = jnp.minimum(large, half - 1)
    return ret + jnp.where(n < max_exact, n, large)


def _bucket_map():
    kj = jnp.arange(3 * BLOCK)[:, None]
    qi = jnp.arange(BLOCK)[None, :]
    rel = kj - BLOCK - qi
    return jnp.where(jnp.abs(rel) <= WINDOW, _t5_bucket(rel), -1).astype(jnp.int32)


def _bias_kernel(relb_ref, bk_ref, o_ref):
    h = pl.program_id(0)
    bk = bk_ref[...]
    acc = jnp.full(bk.shape, NEG_INF, F32)
    for b in range(N_BUCKETS):
        acc = jnp.where(bk == b, relb_ref[b, h], acc)
    key = lax.broadcasted_iota(jnp.int32, bk.shape, 0)
    o_ref[VAR_MID] = acc
    o_ref[VAR_FIRST] = jnp.where(key < BLOCK, NEG_INF, acc)
    o_ref[VAR_LAST] = jnp.where(key >= 2 * BLOCK, NEG_INF, acc)


def _bias_table(rel_bias, casts):
    return _call_with_casts(
        _bias_kernel,
        casts,
        grid=(N_HEADS,),
        in_specs=[
            pl.BlockSpec(memory_space=pltpu.SMEM),
            pl.BlockSpec((3 * BLOCK, BLOCK), lambda h: (0, 0)),
        ],
        out_spec=pl.BlockSpec((3, None, 3 * BLOCK, BLOCK), lambda h: (0, h, 0, 0)),
        out_shape=jax.ShapeDtypeStruct((3, N_HEADS, 3 * BLOCK, BLOCK), F32),
        operands=(rel_bias, _bucket_map()),
        compiler_params=_params(16, ("arbitrary",)),
        name="bias_table",
    )


def _attn_kernel(sink_ref, q_ref, kc_ref, kp_ref, kn_ref, vc_ref, vp_ref, vn_ref, bias_ref,
                 o_ref, klo_ref, khi_ref, vt_ref, *, layer, tq, blocks_per_seq):
    i = pl.program_id(0)
    nqb = tq // BLOCK
    assert 2 * HEAD_DIM == LANES and blocks_per_seq >= 2

    for pair in range(N_KV_HEADS // 2):
        r = 0
        for part in (kp_ref, kc_ref, kn_ref):
            n = part.shape[0]
            t = pltpu.bitcast(part[:, pair * LANES:(pair + 1) * LANES], jnp.uint32)
            moved = pltpu.roll(t, HEAD_DIM, 1)
            low = lax.broadcasted_iota(jnp.int32, t.shape, 1) < HEAD_DIM
            zero = jnp.zeros_like(t)
            for a, lo, hi in ((2 * pair, jnp.where(low, t, zero), jnp.where(low, zero, moved)),
                              (2 * pair + 1, jnp.where(low, moved, zero),
                               jnp.where(low, zero, t))):
                klo_ref[a, r:r + n, :] = pltpu.bitcast(lo, BF16)
                khi_ref[a, r:r + n, :] = pltpu.bitcast(hi, BF16)
            r += n
        c = 0
        for part in (vp_ref, vc_ref, vn_ref):
            for b0 in range(0, part.shape[0], BLOCK):
                blk = part[b0:b0 + BLOCK, pair * LANES:(pair + 1) * LANES].astype(F32)
                blk_t = blk.T.astype(BF16)
                vt_ref[2 * pair, 0:HEAD_DIM, c:c + BLOCK] = blk_t[0:HEAD_DIM, :]
                vt_ref[2 * pair + 1, 0:HEAD_DIM, c:c + BLOCK] = blk_t[HEAD_DIM:, :]
                c += BLOCK
    for a in range(N_KV_HEADS):
        vt_ref[a, HEAD_DIM:, :] = jnp.ones((SUM_ROWS, vt_ref.shape[2]), BF16)

    nt_dims = (((1,), (1,)), ((), ()))
    q_scale = jnp.asarray(HEAD_DIM ** -0.5, BF16)

    def scores(qb, a):
        q0 = qb * BLOCK
        c0 = a * GROUP * HEAD_DIM
        qt = jnp.concatenate([q_ref[q0:q0 + BLOCK, c0:c0 + LANES],
                              q_ref[q0:q0 + BLOCK, c0 + LANES:c0 + 2 * LANES]], axis=0)
        qt = qt * q_scale
        win = slice(q0, q0 + 3 * BLOCK)
        return [lax.dot_general(k_ref[a, win, :], qt, nt_dims, preferred_element_type=F32)
                for k_ref in (klo_ref, khi_ref)]

    def softmax(qb, a, s_par):
        n = lax.rem(i * nqb + qb, blocks_per_seq)
        var = jnp.where(n == 0, VAR_FIRST, jnp.where(n == blocks_per_seq - 1, VAR_LAST, VAR_MID))
        pts, sinks = [], []
        for par in range(2):
            p_t, e_t = [], []
            for tile in range(2):
                h = a * GROUP + 2 * tile + par
                sink = sink_ref[layer, h]
                sg = s_par[par][:, tile * BLOCK:(tile + 1) * BLOCK] + bias_ref[var, h]
                m = jnp.maximum(jnp.max(sg, axis=0, keepdims=True), sink)
                p_t.append(jnp.exp(sg - m).astype(BF16))
                e_t.append(jnp.exp(sink - m))
            pts.append(jnp.concatenate(p_t, axis=1))
            sinks.append(jnp.concatenate(e_t, axis=1))
        return pts, sinks

    def values(qb, a, pts, sinks):
        q0 = qb * BLOCK
        c0 = a * GROUP * HEAD_DIM
        v_t = vt_ref[a, :, q0:q0 + 3 * BLOCK]
        halves = []
        for par in range(2):
            o_full = jnp.dot(v_t, pts[par], preferred_element_type=F32)
            denom = o_full[HEAD_DIM:HEAD_DIM + 1, :] + sinks[par]
            halves.append(o_full[0:HEAD_DIM, :] / denom)
        o_t = jnp.concatenate(halves, axis=0)
        for tile in range(2):
            o_ref[q0:q0 + BLOCK, c0 + tile * LANES:c0 + (tile + 1) * LANES] = (
                o_t[:, tile * BLOCK:(tile + 1) * BLOCK].T.astype(BF16))

    units = [(qb, a) for qb in range(nqb) for a in range(N_KV_HEADS)]
    s_next = scores(*units[0])
    sm_prev = None
    for t in range(len(units) + 1):
        s_cur = s_next
        if t + 1 < len(units):
            s_next = scores(*units[t + 1])
        sm_cur = softmax(*units[t], s_cur) if t < len(units) else None
        if t >= 1:
            values(*units[t - 1], *sm_prev)
        sm_prev = sm_cur


def _attention(proj, bias, sink, layer, casts, *, seq, tq=ATTN_TILE):
    t = proj.shape[0]
    assert seq % tq == 0 and tq % BLOCK == 0
    nb = tq // BLOCK
    last_blk = t // BLOCK - 1
    kvw = N_KV_HEADS * HEAD_DIM
    kcol, vcol = COL_K // kvw, COL_V // kvw
    prev_map = lambda c: (lambda i: (jnp.maximum(i * nb - 1, 0), c))
    next_map = lambda c: (lambda i: (jnp.minimum((i + 1) * nb, last_blk), c))
    k_scratch = pltpu.VMEM((N_KV_HEADS, tq + 2 * BLOCK, LANES), BF16)
    vt_scratch = pltpu.VMEM((N_KV_HEADS, HEAD_DIM + SUM_ROWS, tq + 2 * BLOCK), BF16)
    return _call_with_casts(
        functools.partial(_attn_kernel, layer=layer, tq=tq, blocks_per_seq=seq // BLOCK),
        casts,
        grid=(t // tq,),
        in_specs=[
            pl.BlockSpec(memory_space=pltpu.SMEM),
            pl.BlockSpec((tq, D_MODEL), lambda i: (i, COL_Q // D_MODEL)),
            pl.BlockSpec((tq, kvw), lambda i: (i, kcol)),
            pl.BlockSpec((BLOCK, kvw), prev_map(kcol)),
            pl.BlockSpec((BLOCK, kvw), next_map(kcol)),
            pl.BlockSpec((tq, kvw), lambda i: (i, vcol)),
            pl.BlockSpec((BLOCK, kvw), prev_map(vcol)),
            pl.BlockSpec((BLOCK, kvw), next_map(vcol)),
            _resident((3, N_HEADS, 3 * BLOCK, BLOCK), lambda i: (0, 0, 0, 0)),
        ],
        out_spec=pl.BlockSpec((tq, D_MODEL), lambda i: (i, 0)),
        out_shape=jax.ShapeDtypeStruct((t, D_MODEL), BF16),
        operands=(sink, proj, proj, proj, proj, proj, proj, proj, bias),
        scratch_shapes=[k_scratch, k_scratch, vt_scratch],
        compiler_params=_params(48, ("arbitrary",)),
        name="window_attn",
    )


def _pool_band(rows):
    t_i = jnp.arange(rows)[:, None]
    j_i = jnp.arange(rows + 2 * POOL_HALO)[None, :]
    d = j_i - POOL_HALO - t_i
    bands = [(d >= -(w // 2)) & (d <= w - 1 - w // 2) for w in POOL_WINDOWS]
    return jnp.stack(bands).astype(BF16)


def _mixer_merge_kernel(a_ref, u_ref, up_ref, un_ref, t_ref,
                        ga0_ref, ga1_ref, gp0_ref, gp1_ref, gt0_ref, gt1_ref, x_ref,
                        band_ref, wp_ref, ps_ref, wa_ref, wt_ref, wo_ref,
                        o_ref, *, tiles_per_seq):
    tm = a_ref.shape[0]
    seq = tm * tiles_per_seq
    n = lax.rem(pl.program_id(0), tiles_per_seq)
    first, last = n == 0, n == tiles_per_seq - 1
    cg = POOL_CG
    assert D_MODEL // cg == POOL_GROUPS

    def gate(lo_ref, hi_ref):
        g = jnp.concatenate([lo_ref[...], hi_ref[...]], axis=1)
        return jax.nn.sigmoid(g.astype(F32))

    def window_sum(g):
        cols = slice(g * cg, (g + 1) * cg)
        prev = jnp.where(first, 0.0, up_ref[:, cols].astype(F32)).astype(BF16)
        nxt = jnp.where(last, 0.0, un_ref[:, cols].astype(F32)).astype(BF16)
        ctx = jnp.concatenate([prev, u_ref[:, cols], nxt], axis=0)
        sub = band_ref.shape[1]
        return jnp.concatenate(
            [jnp.dot(band_ref[g], ctx[r0:r0 + sub + 2 * POOL_HALO, :],
                     preferred_element_type=F32) for r0 in range(0, tm, sub)], axis=0)

    t_abs = n * tm + lax.broadcasted_iota(jnp.int32, (tm, 1), 0)

    def pooled(g, s):
        cols = slice(g * cg, (g + 1) * cg)
        win = POOL_WINDOWS[g]
        lo_off, hi_off = win // 2, win - 1 - win // 2
        cnt = jnp.minimum(t_abs + hi_off, seq - 1) - jnp.maximum(t_abs - lo_off, 0) + 1
        p = (s / cnt.astype(F32) - u_ref[:, cols].astype(F32)).astype(BF16)
        return jnp.dot(p, wp_ref[g], preferred_element_type=F32) * ps_ref[:, cols]

    sums = [window_sum(g) for g in range(POOL_GROUPS)]
    ya = jnp.dot(a_ref[...], wa_ref[...], preferred_element_type=F32)
    yp = jnp.concatenate([pooled(g, sums[g]) for g in range(POOL_GROUPS)], axis=1)
    merged = gate(ga0_ref, ga1_ref) * ya + gate(gp0_ref, gp1_ref) * yp
    yt = jnp.dot(t_ref[...], wt_ref[...], preferred_element_type=F32)
    merged = merged + gate(gt0_ref, gt1_ref) * yt
    o_ref[...] = x_ref[...] + jnp.dot(merged.astype(BF16), wo_ref[...],
                                      preferred_element_type=F32)


def _mixer_merge(proj, attn, x, w_pool, pool_scale, w_a_out, w_attn_out, w_o, layer,
                 *, seq, tm=ROW_TILE):
    t = x.shape[0]
    assert seq % tm == 0 and tm % POOL_HALO == 0

    def row(col_block):
        return pl.BlockSpec((tm, D_MODEL), lambda i: (i, col_block))

    def halo(rows, col_block):
        per_tile, last_blk = tm // rows, t // rows - 1
        return (pl.BlockSpec((rows, D_MODEL),
                             lambda i: (jnp.maximum(i * per_tile - 1, 0), col_block)),
                pl.BlockSpec((rows, D_MODEL),
                             lambda i: (jnp.minimum((i + 1) * per_tile, last_blk), col_block)))

    def gate(col):
        def spec(col_block):
            return pl.BlockSpec((tm, GATE_W), lambda i: (i, col_block))
        return [spec(col // GATE_W + half) for half in range(D_MODEL // GATE_W)]

    wspec = _resident((D_MODEL, D_MODEL), lambda i: (0, 0))
    assert tm % POOL_SUB == 0
    band = _pool_band(POOL_SUB)
    u_blk = COL_U // D_MODEL
    return pl.pallas_call(
        functools.partial(_mixer_merge_kernel, tiles_per_seq=seq // tm),
        grid=(t // tm,),
        in_specs=[row(COL_A // D_MODEL),
                  row(u_blk), *halo(POOL_HALO, u_blk),
                  row(0),
                  *gate(COL_GA), *gate(COL_GP), *gate(COL_GT),
                  row(0),
                  _resident(band.shape, lambda i: (0, 0, 0)),
                  _resident((POOL_GROUPS, POOL_CG, POOL_CG), lambda i: (0, 0, 0)),
                  pl.BlockSpec((None, 1, D_MODEL), lambda i: (layer, 0, 0)),
                  wspec, wspec, wspec],
        out_specs=row(0),
        out_shape=jax.ShapeDtypeStruct((t, D_MODEL), F32),
        compiler_params=_params(56, ("parallel",)),
        name="mixer_merge",
    )(proj, proj, proj, proj, attn,
      proj, proj, proj, proj, proj, proj, x,
      band, w_pool, pool_scale, w_a_out, w_attn_out, w_o)


def _ffn_kernel(x_ref, g_ref, wgu_ref, wd_ref, gf_ref, o_ref, *, chunk, final):
    x = x_ref[...]
    h = _rms(x, g_ref[...]).astype(BF16)
    acc = x
    for c0, n in _col_chunks(D_FF, chunk):
        gate = jnp.dot(h, wgu_ref[:, c0:c0 + n], preferred_element_type=F32)
        up = jnp.dot(h, wgu_ref[:, D_FF + c0:D_FF + c0 + n], preferred_element_type=F32)
        act = (jax.nn.silu(gate) * up).astype(BF16)
        acc = acc + jnp.dot(act, wd_ref[c0:c0 + n, :], preferred_element_type=F32)
    o_ref[...] = _rms(acc, gf_ref[...]) if final else acc


def _ffn(x, g_ffn, w_gu, w_down, g_final, layer, casts, *, final, tm=ROW_TILE, chunk=1536):
    t = x.shape[0]
    row = pl.BlockSpec((tm, D_MODEL), lambda i: (i, 0))
    return _call_with_casts(
        functools.partial(_ffn_kernel, chunk=chunk, final=final),
        casts,
        grid=(t // tm,),
        in_specs=[row,
                  pl.BlockSpec((None, 1, D_MODEL), lambda i: (layer, 0, 0)),
                  _resident((D_MODEL, 2 * D_FF), lambda i: (0, 0)),
                  _resident((D_FF, D_MODEL), lambda i: (0, 0)),
                  pl.BlockSpec((1, D_MODEL), lambda i: (0, 0))],
        out_spec=row,
        out_shape=jax.ShapeDtypeStruct((t, D_MODEL), F32),
        operands=(x, g_ffn, w_gu, w_down, g_final),
        compiler_params=_params(56, ("arbitrary",)),
        name="ffn",
    )


@jax.jit
def _trunk(x, w_in, conv_w, w_a_out, w_pool, pool_scale, w_attn_out, attn_sink, w_o,
           g_mix, g_ffn, w_gu, w_down, rel_bias, g_final):
    batch, seq, d = x.shape
    depth = w_in.shape[0]
    xf = x.reshape(batch * seq, d)

    conv_w3 = conv_w.reshape(depth, 3, d)
    pool_scale3 = pool_scale.reshape(depth, 1, d)
    g_mix3 = g_mix.reshape(depth, 1, d)
    g_ffn3 = g_ffn.reshape(depth, 1, d)
    g_final2 = g_final.reshape(1, d)
    w_pool2 = w_pool.reshape(depth, POOL_GROUPS * POOL_CG, POOL_CG)
    steps = (batch * seq) // ROW_TILE
    attn_steps = (batch * seq) // ATTN_TILE
    bias, w_in_l = _bias_table(rel_bias, [_Cast(w_in, 0, d // N_HEADS)])
    for l in range(depth):
        proj, wa_b, wt_b, wo_b, wp_b = _inproj(
            xf, g_mix3, conv_w3, w_in_l, l,
            [_Cast(w, l, d // steps) for w in (w_a_out, w_attn_out, w_o, w_pool2)], seq=seq)
        attn, wgu_b, wd_b = _attention(
            proj, bias, attn_sink, l,
            [_Cast(w_gu, l, d // attn_steps), _Cast(w_down, l, D_FF // W_DOWN_CAST_BLOCKS)],
            seq=seq)
        x1 = _mixer_merge(proj, attn, xf, wp_b.reshape(POOL_GROUPS, POOL_CG, POOL_CG),
                          pool_scale3, wa_b, wt_b, wo_b, l, seq=seq)
        last = l == depth - 1
        xf, *nxt = _ffn(x1, g_ffn3, wgu_b, wd_b, g_final2, l,
                        [] if last else [_Cast(w_in, l + 1, d // steps)], final=last)
        w_in_l = nxt[0] if nxt else None
    return xf.reshape(batch, seq, d)


def kernel(x, w_in, conv_w, w_a_out, w_pool, pool_scale, w_attn_out, attn_sink, w_o, g_mix,
           g_ffn, w_gu, w_down, rel_bias, g_final):
    return _trunk(x, w_in, conv_w, w_a_out, w_pool, pool_scale, w_attn_out, attn_sink, w_o,
                  g_mix, g_ffn, w_gu, w_down, rel_bias, g_final)
```

```python
import functools
import math

import jax
import jax.numpy as jnp
from jax import lax
from jax.experimental import pallas as pl
from jax.experimental.pallas import tpu as pltpu

F32 = jnp.float32
BF16 = jnp.bfloat16

D_MODEL = 1024
N_HEADS = 16
N_KV_HEADS = 4
HEAD_DIM = 64
GROUP = N_HEADS // N_KV_HEADS
WINDOW = 128
BLOCK = 128
N_BUCKETS = 32
MAX_DISTANCE = 128
POOL_GROUPS = 4
POOL_CG = D_MODEL // POOL_GROUPS
D_FF = 2816
EPS = 1e-6
NEG_INF = -1e30

W_COL_B, W_COL_C, W_COL_X, W_COL_REST = 0, 1024, 2048, 3072
IN_TOTAL = 8704
COL_A, COL_U, COL_Q = 0, 1024, 2048
COL_K, COL_V = 3072, 3328
COL_GA, COL_GP, COL_GT = 3584, 4608, 5632
PROJ_W = D_MODEL + IN_TOTAL - W_COL_REST
GATE_W = 512

LANES = 128
MXU_W = 256
ROW_TILE = 512
ATTN_TILE = 1024
W_DOWN_CAST_BLOCKS = 11
BF16_ROWS = 16
SUM_ROWS = BF16_ROWS
CONV_HALO = BF16_ROWS
POOL_HALO = 64
POOL_SUB = 128
POOL_WINDOWS = (2, 4, 8, 16)
MIB = 1024 * 1024

VAR_MID, VAR_FIRST, VAR_LAST = 0, 1, 2


def _params(vmem_mib, sem):
    return pltpu.CompilerParams(dimension_semantics=sem, vmem_limit_bytes=vmem_mib * MIB)


def _resident(shape, index_map):
    return pl.BlockSpec(shape, index_map, pipeline_mode=pl.Buffered(1))


def _rms(x, g):
    ms = jnp.mean(x * x, axis=-1, keepdims=True)
    return (x * lax.rsqrt(ms + EPS)) * g


class _Cast:
    def __init__(self, weights, layer, rows):
        _, r, c = weights.shape
        assert r % rows == 0 and rows % BF16_ROWS == 0
        self.weights, self.layer, self.rows, self.r, self.c = weights, layer, rows, r, c

    def specs(self, n_steps):
        last = self.r // self.rows - 1
        assert last < n_steps
        layer = self.layer
        return (pl.BlockSpec((None, self.rows, self.c),
                             lambda i: (layer, jnp.minimum(i, last), 0)),
                pl.BlockSpec((self.rows, self.c), lambda i: (jnp.minimum(i, last), 0)),
                jax.ShapeDtypeStruct((self.r, self.c), BF16))


def _with_casts(body, n_in, n_casts):
    def kernel(*refs):
        own_in = refs[:n_in]
        cast_src = refs[n_in:n_in + n_casts]
        out = refs[n_in + n_casts]
        cast_dst = refs[n_in + n_casts + 1:n_in + 2 * n_casts + 1]
        scratch = refs[n_in + 2 * n_casts + 1:]
        for src, dst in zip(cast_src, cast_dst):
            dst[...] = src[...].astype(BF16)
        body(*own_in, out, *scratch)
    return kernel


def _call_with_casts(body, casts, *, grid, in_specs, out_spec, out_shape, operands, **kwargs):
    n_steps = grid[0]
    cast_specs = [c.specs(n_steps) for c in casts]
    outs = pl.pallas_call(
        _with_casts(body, len(in_specs), len(casts)),
        grid=grid,
        in_specs=list(in_specs) + [s[0] for s in cast_specs],
        out_specs=[out_spec] + [s[1] for s in cast_specs],
        out_shape=[out_shape] + [s[2] for s in cast_specs],
        **kwargs,
    )(*operands, *[c.weights for c in casts])
    return tuple(outs)


def _col_chunks(total, width):
    assert total % MXU_W == 0 and width % MXU_W == 0
    return [(c, min(width, total - c)) for c in range(0, total, width)]


def _shift(x, k):
    n = x.shape[0]
    return pltpu.roll(x, k % n, 0)


def _inproj_kernel(x_ref, xp_ref, xn_ref, g_ref, cw_ref, w_ref, o_ref,
                   *, chunk, conv_chunk, tiles_per_seq):
    tm = x_ref.shape[0]
    halo = xp_ref.shape[0]
    n = lax.rem(pl.program_id(0), tiles_per_seq)
    g = g_ref[...]
    h = _rms(x_ref[...], g).astype(BF16)
    h_prev = jnp.where(n == 0, 0.0, _rms(xp_ref[...], g)).astype(BF16)
    h_next = jnp.where(n == tiles_per_seq - 1, 0.0, _rms(xn_ref[...], g)).astype(BF16)
    h_ext = jnp.concatenate([h_prev, h, h_next], axis=0)

    for c0, nc in _col_chunks(D_MODEL, conv_chunk):
        def proj(lhs, col):
            return jnp.dot(lhs, w_ref[:, col + c0:col + c0 + nc], preferred_element_type=F32)

        u = proj(h_ext, W_COL_C) * proj(h_ext, W_COL_X)
        w = cw_ref[:, c0:c0 + nc]
        y = _shift(u, 1) * w[0:1, :] + u * w[1:2, :] + _shift(u, -1) * w[2:3, :]
        o_ref[:, COL_A + c0:COL_A + c0 + nc] = (
            proj(h, W_COL_B) * y[halo:halo + tm, :]).astype(BF16)

    for c0, nc in _col_chunks(IN_TOTAL - W_COL_REST, chunk):
        o_ref[:, COL_U + c0:COL_U + c0 + nc] = jnp.dot(
            h, w_ref[:, W_COL_REST + c0:W_COL_REST + c0 + nc],
            preferred_element_type=F32).astype(BF16)


def _inproj(x, g, conv_w, w, layer, casts, *, seq, tm=ROW_TILE, chunk=1024, conv_chunk=256):
    t = x.shape[0]
    assert seq % tm == 0 and tm % CONV_HALO == 0
    per_tile, last_blk = tm // CONV_HALO, t // CONV_HALO - 1
    return _call_with_casts(
        functools.partial(_inproj_kernel, chunk=chunk, conv_chunk=conv_chunk,
                          tiles_per_seq=seq // tm),
        casts,
        grid=(t // tm,),
        in_specs=[
            pl.BlockSpec((tm, D_MODEL), lambda i: (i, 0)),
            pl.BlockSpec((CONV_HALO, D_MODEL), lambda i: (jnp.maximum(i * per_tile - 1, 0), 0)),
            pl.BlockSpec((CONV_HALO, D_MODEL),
                         lambda i: (jnp.minimum((i + 1) * per_tile, last_blk), 0)),
            pl.BlockSpec((None, 1, D_MODEL), lambda i: (layer, 0, 0)),
            pl.BlockSpec((None, 3, D_MODEL), lambda i: (layer, 0, 0)),
            _resident((D_MODEL, IN_TOTAL), lambda i: (0, 0)),
        ],
        out_spec=pl.BlockSpec((tm, PROJ_W), lambda i: (i, 0)),
        out_shape=jax.ShapeDtypeStruct((t, PROJ_W), BF16),
        operands=(x, x, x, g, conv_w, w),
        compiler_params=_params(52, ("arbitrary",)),
        name="inproj",
    )


def _t5_bucket(rel):
    half = N_BUCKETS // 2
    max_exact = half // 2
    ret = jnp.where(rel > 0, half, 0)
    n = jnp.abs(rel)
    nf = jnp.maximum(n, 1).astype(jnp.float32)
    large = max_exact + (jnp.log(nf / max_exact) / math.log(MAX_DISTANCE / max_exact)
                         * (half - max_exact)).astype(jnp.int32)
    large = jnp.minimum(large, half - 1)
    return ret + jnp.where(n < max_exact, n, large)


def _bucket_map():
    kj = jnp.arange(3 * BLOCK)[:, None]
    qi = jnp.arange(BLOCK)[None, :]
    rel = kj - BLOCK - qi
    return jnp.where(jnp.abs(rel) <= WINDOW, _t5_bucket(rel), -1).astype(jnp.int32)


def _bias_kernel(relb_ref, bk_ref, o_ref):
    h = pl.program_id(0)
    bk = bk_ref[...]
    acc = jnp.full(bk.shape, NEG_INF, F32)
    for b in range(N_BUCKETS):
        acc = jnp.where(bk == b, relb_ref[b, h], acc)
    key = lax.broadcasted_iota(jnp.int32, bk.shape, 0)
    o_ref[VAR_MID] = acc
    o_ref[VAR_FIRST] = jnp.where(key < BLOCK, NEG_INF, acc)
    o_ref[VAR_LAST] = jnp.where(key >= 2 * BLOCK, NEG_INF, acc)


def _bias_table(rel_bias, casts):
    return _call_with_casts(
        _bias_kernel,
        casts,
        grid=(N_HEADS,),
        in_specs=[
            pl.BlockSpec(memory_space=pltpu.SMEM),
            pl.BlockSpec((3 * BLOCK, BLOCK), lambda h: (0, 0)),
        ],
        out_spec=pl.BlockSpec((3, None, 3 * BLOCK, BLOCK), lambda h: (0, h, 0, 0)),
        out_shape=jax.ShapeDtypeStruct((3, N_HEADS, 3 * BLOCK, BLOCK), F32),
        operands=(rel_bias, _bucket_map()),
        compiler_params=_params(16, ("arbitrary",)),
        name="bias_table",
    )


def _attn_kernel(sink_ref, q_ref, kc_ref, kp_ref, kn_ref, vc_ref, vp_ref, vn_ref, bias_ref,
                 o_ref, klo_ref, khi_ref, vt_ref, *, layer, tq, blocks_per_seq):
    i = pl.program_id(0)
    nqb = tq // BLOCK
    assert 2 * HEAD_DIM == LANES and blocks_per_seq >= 2

    for pair in range(N_KV_HEADS // 2):
        r = 0
        for part in (kp_ref, kc_ref, kn_ref):
            n = part.shape[0]
            t = pltpu.bitcast(part[:, pair * LANES:(pair + 1) * LANES], jnp.uint32)
            moved = pltpu.roll(t, HEAD_DIM, 1)
            low = lax.broadcasted_iota(jnp.int32, t.shape, 1) < HEAD_DIM
            zero = jnp.zeros_like(t)
            for a, lo, hi in ((2 * pair, jnp.where(low, t, zero), jnp.where(low, zero, moved)),
                              (2 * pair + 1, jnp.where(low, moved, zero),
                               jnp.where(low, zero, t))):
                klo_ref[a, r:r + n, :] = pltpu.bitcast(lo, BF16)
                khi_ref[a, r:r + n, :] = pltpu.bitcast(hi, BF16)
            r += n
        c = 0
        for part in (vp_ref, vc_ref, vn_ref):
            for b0 in range(0, part.shape[0], BLOCK):
                blk = part[b0:b0 + BLOCK, pair * LANES:(pair + 1) * LANES].astype(F32)
                blk_t = blk.T.astype(BF16)
                vt_ref[2 * pair, 0:HEAD_DIM, c:c + BLOCK] = blk_t[0:HEAD_DIM, :]
                vt_ref[2 * pair + 1, 0:HEAD_DIM, c:c + BLOCK] = blk_t[HEAD_DIM:, :]
                c += BLOCK
    for a in range(N_KV_HEADS):
        vt_ref[a, HEAD_DIM:, :] = jnp.ones((SUM_ROWS, vt_ref.shape[2]), BF16)

    nt_dims = (((1,), (1,)), ((), ()))
    q_scale = jnp.asarray(HEAD_DIM ** -0.5, BF16)

    def scores(qb, a):
        q0 = qb * BLOCK
        c0 = a * GROUP * HEAD_DIM
        qt = jnp.concatenate([q_ref[q0:q0 + BLOCK, c0:c0 + LANES],
                              q_ref[q0:q0 + BLOCK, c0 + LANES:c0 + 2 * LANES]], axis=0)
        qt = qt * q_scale
        win = slice(q0, q0 + 3 * BLOCK)
        return [lax.dot_general(k_ref[a, win, :], qt, nt_dims, preferred_element_type=F32)
                for k_ref in (klo_ref, khi_ref)]

    def softmax(qb, a, s_par):
        n = lax.rem(i * nqb + qb, blocks_per_seq)
        var = jnp.where(n == 0, VAR_FIRST, jnp.where(n == blocks_per_seq - 1, VAR_LAST, VAR_MID))
        pts, sinks = [], []
        for par in range(2):
            p_t, e_t = [], []
            for tile in range(2):
                h = a * GROUP + 2 * tile + par
                sink = sink_ref[layer, h]
                sg = s_par[par][:, tile * BLOCK:(tile + 1) * BLOCK] + bias_ref[var, h]
                m = jnp.maximum(jnp.max(sg, axis=0, keepdims=True), sink)
                p_t.append(jnp.exp(sg - m).astype(BF16))
                e_t.append(jnp.exp(sink - m))
            pts.append(jnp.concatenate(p_t, axis=1))
            sinks.append(jnp.concatenate(e_t, axis=1))
        return pts, sinks

    def values(qb, a, pts, sinks):
        q0 = qb * BLOCK
        c0 = a * GROUP * HEAD_DIM
        v_t = vt_ref[a, :, q0:q0 + 3 * BLOCK]
        halves = []
        for par in range(2):
            o_full = jnp.dot(v_t, pts[par], preferred_element_type=F32)
            denom = o_full[HEAD_DIM:HEAD_DIM + 1, :] + sinks[par]
            halves.append(o_full[0:HEAD_DIM, :] / denom)
        o_t = jnp.concatenate(halves, axis=0)
        for tile in range(2):
            o_ref[q0:q0 + BLOCK, c0 + tile * LANES:c0 + (tile + 1) * LANES] = (
                o_t[:, tile * BLOCK:(tile + 1) * BLOCK].T.astype(BF16))

    units = [(qb, a) for qb in range(nqb) for a in range(N_KV_HEADS)]
    s_next = scores(*units[0])
    sm_prev = None
    for t in range(len(units) + 1):
        s_cur = s_next
        if t + 1 < len(units):
            s_next = scores(*units[t + 1])
        sm_cur = softmax(*units[t], s_cur) if t < len(units) else None
        if t >= 1:
            values(*units[t - 1], *sm_prev)
        sm_prev = sm_cur


def _attention(proj, bias, sink, layer, casts, *, seq, tq=ATTN_TILE):
    t = proj.shape[0]
    assert seq % tq == 0 and tq % BLOCK == 0
    nb = tq // BLOCK
    last_blk = t // BLOCK - 1
    kvw = N_KV_HEADS * HEAD_DIM
    kcol, vcol = COL_K // kvw, COL_V // kvw
    prev_map = lambda c: (lambda i: (jnp.maximum(i * nb - 1, 0), c))
    next_map = lambda c: (lambda i: (jnp.minimum((i + 1) * nb, last_blk), c))
    k_scratch = pltpu.VMEM((N_KV_HEADS, tq + 2 * BLOCK, LANES), BF16)
    vt_scratch = pltpu.VMEM((N_KV_HEADS, HEAD_DIM + SUM_ROWS, tq + 2 * BLOCK), BF16)
    return _call_with_casts(
        functools.partial(_attn_kernel, layer=layer, tq=tq, blocks_per_seq=seq // BLOCK),
        casts,
        grid=(t // tq,),
        in_specs=[
            pl.BlockSpec(memory_space=pltpu.SMEM),
            pl.BlockSpec((tq, D_MODEL), lambda i: (i, COL_Q // D_MODEL)),
            pl.BlockSpec((tq, kvw), lambda i: (i, kcol)),
            pl.BlockSpec((BLOCK, kvw), prev_map(kcol)),
            pl.BlockSpec((BLOCK, kvw), next_map(kcol)),
            pl.BlockSpec((tq, kvw), lambda i: (i, vcol)),
            pl.BlockSpec((BLOCK, kvw), prev_map(vcol)),
            pl.BlockSpec((BLOCK, kvw), next_map(vcol)),
            _resident((3, N_HEADS, 3 * BLOCK, BLOCK), lambda i: (0, 0, 0, 0)),
        ],
        out_spec=pl.BlockSpec((tq, D_MODEL), lambda i: (i, 0)),
        out_shape=jax.ShapeDtypeStruct((t, D_MODEL), BF16),
        operands=(sink, proj, proj, proj, proj, proj, proj, proj, bias),
        scratch_shapes=[k_scratch, k_scratch, vt_scratch],
        compiler_params=_params(48, ("arbitrary",)),
        name="window_attn",
    )


def _pool_band(rows):
    t_i = jnp.arange(rows)[:, None]
    j_i = jnp.arange(rows + 2 * POOL_HALO)[None, :]
    d = j_i - POOL_HALO - t_i
    bands = [(d >= -(w // 2)) & (d <= w - 1 - w // 2) for w in POOL_WINDOWS]
    return jnp.stack(bands).astype(BF16)


def _mixer_merge_kernel(a_ref, u_ref, up_ref, un_ref, t_ref,
                        ga0_ref, ga1_ref, gp0_ref, gp1_ref, gt0_ref, gt1_ref, x_ref,
                        band_ref, wp_ref, ps_ref, wa_ref, wt_ref, wo_ref,
                        o_ref, *, tiles_per_seq):
    tm = a_ref.shape[0]
    seq = tm * tiles_per_seq
    n = lax.rem(pl.program_id(0), tiles_per_seq)
    first, last = n == 0, n == tiles_per_seq - 1
    cg = POOL_CG
    assert D_MODEL // cg == POOL_GROUPS

    def gate(lo_ref, hi_ref):
        g = jnp.concatenate([lo_ref[...], hi_ref[...]], axis=1)
        return jax.nn.sigmoid(g.astype(F32))

    def window_sum(g):
        cols = slice(g * cg, (g + 1) * cg)
        prev = jnp.where(first, 0.0, up_ref[:, cols].astype(F32)).astype(BF16)
        nxt = jnp.where(last, 0.0, un_ref[:, cols].astype(F32)).astype(BF16)
        ctx = jnp.concatenate([prev, u_ref[:, cols], nxt], axis=0)
        sub = band_ref.shape[1]
        return jnp.concatenate(
            [jnp.dot(band_ref[g], ctx[r0:r0 + sub + 2 * POOL_HALO, :],
                     preferred_element_type=F32) for r0 in range(0, tm, sub)], axis=0)

    t_abs = n * tm + lax.broadcasted_iota(jnp.int32, (tm, 1), 0)

    def pooled(g, s):
        cols = slice(g * cg, (g + 1) * cg)
        win = POOL_WINDOWS[g]
        lo_off, hi_off = win // 2, win - 1 - win // 2
        cnt = jnp.minimum(t_abs + hi_off, seq - 1) - jnp.maximum(t_abs - lo_off, 0) + 1
        p = (s / cnt.astype(F32) - u_ref[:, cols].astype(F32)).astype(BF16)
        return jnp.dot(p, wp_ref[g], preferred_element_type=F32) * ps_ref[:, cols]

    sums = [window_sum(g) for g in range(POOL_GROUPS)]
    ya = jnp.dot(a_ref[...], wa_ref[...], preferred_element_type=F32)
    yp = jnp.concatenate([pooled(g, sums[g]) for g in range(POOL_GROUPS)], axis=1)
    merged = gate(ga0_ref, ga1_ref) * ya + gate(gp0_ref, gp1_ref) * yp
    yt = jnp.dot(t_ref[...], wt_ref[...], preferred_element_type=F32)
    merged = merged + gate(gt0_ref, gt1_ref) * yt
    o_ref[...] = x_ref[...] + jnp.dot(merged.astype(BF16), wo_ref[...],
                                      preferred_element_type=F32)


def _mixer_merge(proj, attn, x, w_pool, pool_scale, w_a_out, w_attn_out, w_o, layer,
                 *, seq, tm=ROW_TILE):
    t = x.shape[0]
    assert seq % tm == 0 and tm % POOL_HALO == 0

    def row(col_block):
        return pl.BlockSpec((tm, D_MODEL), lambda i: (i, col_block))

    def halo(rows, col_block):
        per_tile, last_blk = tm // rows, t // rows - 1
        return (pl.BlockSpec((rows, D_MODEL),
                             lambda i: (jnp.maximum(i * per_tile - 1, 0), col_block)),
                pl.BlockSpec((rows, D_MODEL),
                             lambda i: (jnp.minimum((i + 1) * per_tile, last_blk), col_block)))

    def gate(col):
        def spec(col_block):
            return pl.BlockSpec((tm, GATE_W), lambda i: (i, col_block))
        return [spec(col // GATE_W + half) for half in range(D_MODEL // GATE_W)]

    wspec = _resident((D_MODEL, D_MODEL), lambda i: (0, 0))
    assert tm % POOL_SUB == 0
    band = _pool_band(POOL_SUB)
    u_blk = COL_U // D_MODEL
    return pl.pallas_call(
        functools.partial(_mixer_merge_kernel, tiles_per_seq=seq // tm),
        grid=(t // tm,),
        in_specs=[row(COL_A // D_MODEL),
                  row(u_blk), *halo(POOL_HALO, u_blk),
                  row(0),
                  *gate(COL_GA), *gate(COL_GP), *gate(COL_GT),
                  row(0),
                  _resident(band.shape, lambda i: (0, 0, 0)),
                  _resident((POOL_GROUPS, POOL_CG, POOL_CG), lambda i: (0, 0, 0)),
                  pl.BlockSpec((None, 1, D_MODEL), lambda i: (layer, 0, 0)),
                  wspec, wspec, wspec],
        out_specs=row(0),
        out_shape=jax.ShapeDtypeStruct((t, D_MODEL), F32),
        compiler_params=_params(56, ("parallel",)),
        name="mixer_merge",
    )(proj, proj, proj, proj, attn,
      proj, proj, proj, proj, proj, proj, x,
      band, w_pool, pool_scale, w_a_out, w_attn_out, w_o)


def _ffn_kernel(x_ref, g_ref, wgu_ref, wd_ref, gf_ref, o_ref, *, chunk, final):
    x = x_ref[...]
    h = _rms(x, g_ref[...]).astype(BF16)
    acc = x
    for c0, n in _col_chunks(D_FF, chunk):
        gate = jnp.dot(h, wgu_ref[:, c0:c0 + n], preferred_element_type=F32)
        up = jnp.dot(h, wgu_ref[:, D_FF + c0:D_FF + c0 + n], preferred_element_type=F32)
        act = (jax.nn.silu(gate) * up).astype(BF16)
        acc = acc + jnp.dot(act, wd_ref[c0:c0 + n, :], preferred_element_type=F32)
    o_ref[...] = _rms(acc, gf_ref[...]) if final else acc


def _ffn(x, g_ffn, w_gu, w_down, g_final, layer, casts, *, final, tm=ROW_TILE, chunk=1536):
    t = x.shape[0]
    row = pl.BlockSpec((tm, D_MODEL), lambda i: (i, 0))
    return _call_with_casts(
        functools.partial(_ffn_kernel, chunk=chunk, final=final),
        casts,
        grid=(t // tm,),
        in_specs=[row,
                  pl.BlockSpec((None, 1, D_MODEL), lambda i: (layer, 0, 0)),
                  _resident((D_MODEL, 2 * D_FF), lambda i: (0, 0)),
                  _resident((D_FF, D_MODEL), lambda i: (0, 0)),
                  pl.BlockSpec((1, D_MODEL), lambda i: (0, 0))],
        out_spec=row,
        out_shape=jax.ShapeDtypeStruct((t, D_MODEL), F32),
        operands=(x, g_ffn, w_gu, w_down, g_final),
        compiler_params=_params(56, ("arbitrary",)),
        name="ffn",
    )


@jax.jit
def _trunk(x, w_in, conv_w, w_a_out, w_pool, pool_scale, w_attn_out, attn_sink, w_o,
           g_mix, g_ffn, w_gu, w_down, rel_bias, g_final):
    batch, seq, d = x.shape
    depth = w_in.shape[0]
    xf = x.reshape(batch * seq, d)

    conv_w3 = conv_w.reshape(depth, 3, d)
    pool_scale3 = pool_scale.reshape(depth, 1, d)
    g_mix3 = g_mix.reshape(depth, 1, d)
    g_ffn3 = g_ffn.reshape(depth, 1, d)
    g_final2 = g_final.reshape(1, d)
    w_pool2 = w_pool.reshape(depth, POOL_GROUPS * POOL_CG, POOL_CG)
    steps = (batch * seq) // ROW_TILE
    attn_steps = (batch * seq) // ATTN_TILE
    bias, w_in_l = _bias_table(rel_bias, [_Cast(w_in, 0, d // N_HEADS)])
    for l in range(depth):
        proj, wa_b, wt_b, wo_b, wp_b = _inproj(
            xf, g_mix3, conv_w3, w_in_l, l,
            [_Cast(w, l, d // steps) for w in (w_a_out, w_attn_out, w_o, w_pool2)], seq=seq)
        attn, wgu_b, wd_b = _attention(
            proj, bias, attn_sink, l,
            [_Cast(w_gu, l, d // attn_steps), _Cast(w_down, l, D_FF // W_DOWN_CAST_BLOCKS)],
            seq=seq)
        x1 = _mixer_merge(proj, attn, xf, wp_b.reshape(POOL_GROUPS, POOL_CG, POOL_CG),
                          pool_scale3, wa_b, wt_b, wo_b, l, seq=seq)
        last = l == depth - 1
        xf, *nxt = _ffn(x1, g_ffn3, wgu_b, wd_b, g_final2, l,
                        [] if last else [_Cast(w_in, l + 1, d // steps)], final=last)
        w_in_l = nxt[0] if nxt else None
    return xf.reshape(batch, seq, d)


def kernel(x, w_in, conv_w, w_a_out, w_pool, pool_scale, w_attn_out, attn_sink, w_o, g_mix,
           g_ffn, w_gu, w_down, rel_bias, g_final):
    return _trunk(x, w_in, conv_w, w_a_out, w_pool, pool_scale, w_attn_out, attn_sink, w_o,
                  g_mix, g_ffn, w_gu, w_down, rel_bias, g_final)
```

```python
import functools
import math

import jax
import jax.numpy as jnp
from jax import lax
from jax.experimental import pallas as pl
from jax.experimental.pallas import tpu as pltpu

F32 = jnp.float32
BF16 = jnp.bfloat16

D_MODEL = 1024
N_HEADS = 16
N_KV_HEADS = 4
HEAD_DIM = 64
GROUP = N_HEADS // N_KV_HEADS
WINDOW = 128
BLOCK = 128
N_BUCKETS = 32
MAX_DISTANCE = 128
POOL_GROUPS = 4
POOL_CG = D_MODEL // POOL_GROUPS
D_FF = 2816
EPS = 1e-6
NEG_INF = -1e30

W_COL_B, W_COL_C, W_COL_X, W_COL_REST = 0, 1024, 2048, 3072
IN_TOTAL = 8704
COL_A, COL_U, COL_Q = 0, 1024, 2048
COL_K, COL_V = 3072, 3328
COL_GA, COL_GP, COL_GT = 3584, 4608, 5632
PROJ_W = D_MODEL + IN_TOTAL - W_COL_REST
GATE_W = 512

LANES = 128
MXU_W = 256
ROW_TILE = 512
ATTN_TILE = 1024
W_DOWN_CAST_BLOCKS = 11
BF16_ROWS = 16
SUM_ROWS = BF16_ROWS
CONV_HALO = BF16_ROWS
POOL_HALO = 64
POOL_SUB = 128
POOL_WINDOWS = (2, 4, 8, 16)
MIB = 1024 * 1024

VAR_MID, VAR_FIRST, VAR_LAST = 0, 1, 2


def _params(vmem_mib, sem):
    return pltpu.CompilerParams(dimension_semantics=sem, vmem_limit_bytes=vmem_mib * MIB)


def _resident(shape, index_map):
    return pl.BlockSpec(shape, index_map, pipeline_mode=pl.Buffered(1))


def _rms(x, g):
    ms = jnp.mean(x * x, axis=-1, keepdims=True)
    return (x * lax.rsqrt(ms + EPS)) * g


class _Cast:
    def __init__(self, weights, layer, rows):
        _, r, c = weights.shape
        assert r % rows == 0 and rows % BF16_ROWS == 0
        self.weights, self.layer, self.rows, self.r, self.c = weights, layer, rows, r, c

    def specs(self, n_steps):
        last = self.r // self.rows - 1
        assert last < n_steps
        layer = self.layer
        return (pl.BlockSpec((None, self.rows, self.c),
                             lambda i: (layer, jnp.minimum(i, last), 0)),
                pl.BlockSpec((self.rows, self.c), lambda i: (jnp.minimum(i, last), 0)),
                jax.ShapeDtypeStruct((self.r, self.c), BF16))


def _with_casts(body, n_in, n_casts):
    def kernel(*refs):
        own_in = refs[:n_in]
        cast_src = refs[n_in:n_in + n_casts]
        out = refs[n_in + n_casts]
        cast_dst = refs[n_in + n_casts + 1:n_in + 2 * n_casts + 1]
        scratch = refs[n_in + 2 * n_casts + 1:]
        for src, dst in zip(cast_src, cast_dst):
            dst[...] = src[...].astype(BF16)
        body(*own_in, out, *scratch)
    return kernel


def _call_with_casts(body, casts, *, grid, in_specs, out_spec, out_shape, operands, **kwargs):
    n_steps = grid[0]
    cast_specs = [c.specs(n_steps) for c in casts]
    outs = pl.pallas_call(
        _with_casts(body, len(in_specs), len(casts)),
        grid=grid,
        in_specs=list(in_specs) + [s[0] for s in cast_specs],
        out_specs=[out_spec] + [s[1] for s in cast_specs],
        out_shape=[out_shape] + [s[2] for s in cast_specs],
        **kwargs,
    )(*operands, *[c.weights for c in casts])
    return tuple(outs)


def _col_chunks(total, width):
    assert total % MXU_W == 0 and width % MXU_W == 0
    return [(c, min(width, total - c)) for c in range(0, total, width)]


def _shift(x, k):
    n = x.shape[0]
    return pltpu.roll(x, k % n, 0)


def _inproj_kernel(x_ref, xp_ref, xn_ref, g_ref, cw_ref, w_ref, o_ref,
                   *, chunk, conv_chunk, tiles_per_seq):
    tm = x_ref.shape[0]
    halo = xp_ref.shape[0]
    n = lax.rem(pl.program_id(0), tiles_per_seq)
    g = g_ref[...]
    h = _rms(x_ref[...], g).astype(BF16)
    h_prev = jnp.where(n == 0, 0.0, _rms(xp_ref[...], g)).astype(BF16)
    h_next = jnp.where(n == tiles_per_seq - 1, 0.0, _rms(xn_ref[...], g)).astype(BF16)
    h_ext = jnp.concatenate([h_prev, h, h_next], axis=0)

    for c0, nc in _col_chunks(D_MODEL, conv_chunk):
        def proj(lhs, col):
            return jnp.dot(lhs, w_ref[:, col + c0:col + c0 + nc], preferred_element_type=F32)

        u = proj(h_ext, W_COL_C) * proj(h_ext, W_COL_X)
        w = cw_ref[:, c0:c0 + nc]
        y = _shift(u, 1) * w[0:1, :] + u * w[1:2, :] + _shift(u, -1) * w[2:3, :]
        o_ref[:, COL_A + c0:COL_A + c0 + nc] = (
            proj(h, W_COL_B) * y[halo:halo + tm, :]).astype(BF16)

    for c0, nc in _col_chunks(IN_TOTAL - W_COL_REST, chunk):
        o_ref[:, COL_U + c0:COL_U + c0 + nc] = jnp.dot(
            h, w_ref[:, W_COL_REST + c0:W_COL_REST + c0 + nc],
            preferred_element_type=F32).astype(BF16)


def _inproj(x, g, conv_w, w, layer, casts, *, seq, tm=ROW_TILE, chunk=1024, conv_chunk=256):
    t = x.shape[0]
    assert seq % tm == 0 and tm % CONV_HALO == 0
    per_tile, last_blk = tm // CONV_HALO, t // CONV_HALO - 1
    return _call_with_casts(
        functools.partial(_inproj_kernel, chunk=chunk, conv_chunk=conv_chunk,
                          tiles_per_seq=seq // tm),
        casts,
        grid=(t // tm,),
        in_specs=[
            pl.BlockSpec((tm, D_MODEL), lambda i: (i, 0)),
            pl.BlockSpec((CONV_HALO, D_MODEL), lambda i: (jnp.maximum(i * per_tile - 1, 0), 0)),
            pl.BlockSpec((CONV_HALO, D_MODEL),
                         lambda i: (jnp.minimum((i + 1) * per_tile, last_blk), 0)),
            pl.BlockSpec((None, 1, D_MODEL), lambda i: (layer, 0, 0)),
            pl.BlockSpec((None, 3, D_MODEL), lambda i: (layer, 0, 0)),
            _resident((D_MODEL, IN_TOTAL), lambda i: (0, 0)),
        ],
        out_spec=pl.BlockSpec((tm, PROJ_W), lambda i: (i, 0)),
        out_shape=jax.ShapeDtypeStruct((t, PROJ_W), BF16),
        operands=(x, x, x, g, conv_w, w),
        compiler_params=_params(52, ("arbitrary",)),
        name="inproj",
    )


def _t5_bucket(rel):
    half = N_BUCKETS // 2
    max_exact = half // 2
    ret = jnp.where(rel > 0, half, 0)
    n = jnp.abs(rel)
    nf = jnp.maximum(n, 1).astype(jnp.float32)
    large = max_exact + (jnp.log(nf / max_exact) / math.log(MAX_DISTANCE / max_exact)
                         * (half - max_exact)).astype(jnp.int32)
    large = jnp.minimum(large, half - 1)
    return ret + jnp.where(n < max_exact, n, large)


def _bucket_map():
    kj = jnp.arange(3 * BLOCK)[:, None]
    qi = jnp.arange(BLOCK)[None, :]
    rel = kj - BLOCK - qi
    return jnp.where(jnp.abs(rel) <= WINDOW, _t5_bucket(rel), -1).astype(jnp.int32)


def _bias_kernel(relb_ref, bk_ref, o_ref):
    h = pl.program_id(0)
    bk = bk_ref[...]
    acc = jnp.full(bk.shape, NEG_INF, F32)
    for b in range(N_BUCKETS):
        acc = jnp.where(bk == b, relb_ref[b, h], acc)
    key = lax.broadcasted_iota(jnp.int32, bk.shape, 0)
    o_ref[VAR_MID] = acc
    o_ref[VAR_FIRST] = jnp.where(key < BLOCK, NEG_INF, acc)
    o_ref[VAR_LAST] = jnp.where(key >= 2 * BLOCK, NEG_INF, acc)


def _bias_table(rel_bias, casts):
    return _call_with_casts(
        _bias_kernel,
        casts,
        grid=(N_HEADS,),
        in_specs=[
            pl.BlockSpec(memory_space=pltpu.SMEM),
            pl.BlockSpec((3 * BLOCK, BLOCK), lambda h: (0, 0)),
        ],
        out_spec=pl.BlockSpec((3, None, 3 * BLOCK, BLOCK), lambda h: (0, h, 0, 0)),
        out_shape=jax.ShapeDtypeStruct((3, N_HEADS, 3 * BLOCK, BLOCK), F32),
        operands=(rel_bias, _bucket_map()),
        compiler_params=_params(16, ("arbitrary",)),
        name="bias_table",
    )


def _attn_kernel(sink_ref, q_ref, kc_ref, kp_ref, kn_ref, vc_ref, vp_ref, vn_ref, bias_ref,
                 o_ref, klo_ref, khi_ref, vt_ref, *, layer, tq, blocks_per_seq):
    i = pl.program_id(0)
    nqb = tq // BLOCK
    assert 2 * HEAD_DIM == LANES and blocks_per_seq >= 2

    for pair in range(N_KV_HEADS // 2):
        r = 0
        for part in (kp_ref, kc_ref, kn_ref):
            n = part.shape[0]
            t = pltpu.bitcast(part[:, pair * LANES:(pair + 1) * LANES], jnp.uint32)
            moved = pltpu.roll(t, HEAD_DIM, 1)
            low = lax.broadcasted_iota(jnp.int32, t.shape, 1) < HEAD_DIM
            zero = jnp.zeros_like(t)
            for a, lo, hi in ((2 * pair, jnp.where(low, t, zero), jnp.where(low, zero, moved)),
                              (2 * pair + 1, jnp.where(low, moved, zero),
                               jnp.where(low, zero, t))):
                klo_ref[a, r:r + n, :] = pltpu.bitcast(lo, BF16)
                khi_ref[a, r:r + n, :] = pltpu.bitcast(hi, BF16)
            r += n
        c = 0
        for part in (vp_ref, vc_ref, vn_ref):
            for b0 in range(0, part.shape[0], BLOCK):
                blk = part[b0:b0 + BLOCK, pair * LANES:(pair + 1) * LANES].astype(F32)
                blk_t = blk.T.astype(BF16)
                vt_ref[2 * pair, 0:HEAD_DIM, c:c + BLOCK] = blk_t[0:HEAD_DIM, :]
                vt_ref[2 * pair + 1, 0:HEAD_DIM, c:c + BLOCK] = blk_t[HEAD_DIM:, :]
                c += BLOCK
    for a in range(N_KV_HEADS):
        vt_ref[a, HEAD_DIM:, :] = jnp.ones((SUM_ROWS, vt_ref.shape[2]), BF16)

    nt_dims = (((1,), (1,)), ((), ()))
    q_scale = jnp.asarray(HEAD_DIM ** -0.5, BF16)

    def scores(qb, a):
        q0 = qb * BLOCK
        c0 = a * GROUP * HEAD_DIM
        qt = jnp.concatenate([q_ref[q0:q0 + BLOCK, c0:c0 + LANES],
                              q_ref[q0:q0 + BLOCK, c0 + LANES:c0 + 2 * LANES]], axis=0)
        qt = qt * q_scale
        win = slice(q0, q0 + 3 * BLOCK)
        return [lax.dot_general(k_ref[a, win, :], qt, nt_dims, preferred_element_type=F32)
                for k_ref in (klo_ref, khi_ref)]

    def softmax(qb, a, s_par):
        n = lax.rem(i * nqb + qb, blocks_per_seq)
        var = jnp.where(n == 0, VAR_FIRST, jnp.where(n == blocks_per_seq - 1, VAR_LAST, VAR_MID))
        pts, sinks = [], []
        for par in range(2):
            p_t, e_t = [], []
            for tile in range(2):
                h = a * GROUP + 2 * tile + par
                sink = sink_ref[layer, h]
                sg = s_par[par][:, tile * BLOCK:(tile + 1) * BLOCK] + bias_ref[var, h]
                m = jnp.maximum(jnp.max(sg, axis=0, keepdims=True), sink)
                p_t.append(jnp.exp(sg - m).astype(BF16))
                e_t.append(jnp.exp(sink - m))
            pts.append(jnp.concatenate(p_t, axis=1))
            sinks.append(jnp.concatenate(e_t, axis=1))
        return pts, sinks

    def values(qb, a, pts, sinks):
        q0 = qb * BLOCK
        c0 = a * GROUP * HEAD_DIM
        v_t = vt_ref[a, :, q0:q0 + 3 * BLOCK]
        halves = []
        for par in range(2):
            o_full = jnp.dot(v_t, pts[par], preferred_element_type=F32)
            denom = o_full[HEAD_DIM:HEAD_DIM + 1, :] + sinks[par]
            halves.append(o_full[0:HEAD_DIM, :] / denom)
        o_t = jnp.concatenate(halves, axis=0)
        for tile in range(2):
            o_ref[q0:q0 + BLOCK, c0 + tile * LANES:c0 + (tile + 1) * LANES] = (
                o_t[:, tile * BLOCK:(tile + 1) * BLOCK].T.astype(BF16))

    units = [(qb, a) for qb in range(nqb) for a in range(N_KV_HEADS)]
    s_next = scores(*units[0])
    sm_prev = None
    for t in range(len(units) + 1):
        s_cur = s_next
        if t + 1 < len(units):
            s_next = scores(*units[t + 1])
        sm_cur = softmax(*units[t], s_cur) if t < len(units) else None
        if t >= 1:
            values(*units[t - 1], *sm_prev)
        sm_prev = sm_cur


def _attention(proj, bias, sink, layer, casts, *, seq, tq=ATTN_TILE):
    t = proj.shape[0]
    assert seq % tq == 0 and tq % BLOCK == 0
    nb = tq // BLOCK
    last_blk = t // BLOCK - 1
    kvw = N_KV_HEADS * HEAD_DIM
    kcol, vcol = COL_K // kvw, COL_V // kvw
    prev_map = lambda c: (lambda i: (jnp.maximum(i * nb - 1, 0), c))
    next_map = lambda c: (lambda i: (jnp.minimum((i + 1) * nb, last_blk), c))
    k_scratch = pltpu.VMEM((N_KV_HEADS, tq + 2 * BLOCK, LANES), BF16)
    vt_scratch = pltpu.VMEM((N_KV_HEADS, HEAD_DIM + SUM_ROWS, tq + 2 * BLOCK), BF16)
    return _call_with_casts(
        functools.partial(_attn_kernel, layer=layer, tq=tq, blocks_per_seq=seq // BLOCK),
        casts,
        grid=(t // tq,),
        in_specs=[
            pl.BlockSpec(memory_space=pltpu.SMEM),
            pl.BlockSpec((tq, D_MODEL), lambda i: (i, COL_Q // D_MODEL)),
            pl.BlockSpec((tq, kvw), lambda i: (i, kcol)),
            pl.BlockSpec((BLOCK, kvw), prev_map(kcol)),
            pl.BlockSpec((BLOCK, kvw), next_map(kcol)),
            pl.BlockSpec((tq, kvw), lambda i: (i, vcol)),
            pl.BlockSpec((BLOCK, kvw), prev_map(vcol)),
            pl.BlockSpec((BLOCK, kvw), next_map(vcol)),
            _resident((3, N_HEADS, 3 * BLOCK, BLOCK), lambda i: (0, 0, 0, 0)),
        ],
        out_spec=pl.BlockSpec((tq, D_MODEL), lambda i: (i, 0)),
        out_shape=jax.ShapeDtypeStruct((t, D_MODEL), BF16),
        operands=(sink, proj, proj, proj, proj, proj, proj, proj, bias),
        scratch_shapes=[k_scratch, k_scratch, vt_scratch],
        compiler_params=_params(48, ("arbitrary",)),
        name="window_attn",
    )


def _pool_band(rows):
    t_i = jnp.arange(rows)[:, None]
    j_i = jnp.arange(rows + 2 * POOL_HALO)[None, :]
    d = j_i - POOL_HALO - t_i
    bands = [(d >= -(w // 2)) & (d <= w - 1 - w // 2) for w in POOL_WINDOWS]
    return jnp.stack(bands).astype(BF16)


def _mixer_merge_kernel(a_ref, u_ref, up_ref, un_ref, t_ref,
                        ga0_ref, ga1_ref, gp0_ref, gp1_ref, gt0_ref, gt1_ref, x_ref,
                        band_ref, wp_ref, ps_ref, wa_ref, wt_ref, wo_ref,
                        o_ref, *, tiles_per_seq):
    tm = a_ref.shape[0]
    seq = tm * tiles_per_seq
    n = lax.rem(pl.program_id(0), tiles_per_seq)
    first, last = n == 0, n == tiles_per_seq - 1
    cg = POOL_CG
    assert D_MODEL // cg == POOL_GROUPS

    def gate(lo_ref, hi_ref):
        g = jnp.concatenate([lo_ref[...], hi_ref[...]], axis=1)
        return jax.nn.sigmoid(g.astype(F32))

    def window_sum(g):
        cols = slice(g * cg, (g + 1) * cg)
        prev = jnp.where(first, 0.0, up_ref[:, cols].astype(F32)).astype(BF16)
        nxt = jnp.where(last, 0.0, un_ref[:, cols].astype(F32)).astype(BF16)
        ctx = jnp.concatenate([prev, u_ref[:, cols], nxt], axis=0)
        sub = band_ref.shape[1]
        return jnp.concatenate(
            [jnp.dot(band_ref[g], ctx[r0:r0 + sub + 2 * POOL_HALO, :],
                     preferred_element_type=F32) for r0 in range(0, tm, sub)], axis=0)

    t_abs = n * tm + lax.broadcasted_iota(jnp.int32, (tm, 1), 0)

    def pooled(g, s):
        cols = slice(g * cg, (g + 1) * cg)
        win = POOL_WINDOWS[g]
        lo_off, hi_off = win // 2, win - 1 - win // 2
        cnt = jnp.minimum(t_abs + hi_off, seq - 1) - jnp.maximum(t_abs - lo_off, 0) + 1
        p = (s / cnt.astype(F32) - u_ref[:, cols].astype(F32)).astype(BF16)
        return jnp.dot(p, wp_ref[g], preferred_element_type=F32) * ps_ref[:, cols]

    sums = [window_sum(g) for g in range(POOL_GROUPS)]
    ya = jnp.dot(a_ref[...], wa_ref[...], preferred_element_type=F32)
    yp = jnp.concatenate([pooled(g, sums[g]) for g in range(POOL_GROUPS)], axis=1)
    merged = gate(ga0_ref, ga1_ref) * ya + gate(gp0_ref, gp1_ref) * yp
    yt = jnp.dot(t_ref[...], wt_ref[...], preferred_element_type=F32)
    merged = merged + gate(gt0_ref, gt1_ref) * yt
    o_ref[...] = x_ref[...] + jnp.dot(merged.astype(BF16), wo_ref[...],
                                      preferred_element_type=F32)


def _mixer_merge(proj, attn, x, w_pool, pool_scale, w_a_out, w_attn_out, w_o, layer,
                 *, seq, tm=ROW_TILE):
    t = x.shape[0]
    assert seq % tm == 0 and tm % POOL_HALO == 0

    def row(col_block):
        return pl.BlockSpec((tm, D_MODEL), lambda i: (i, col_block))

    def halo(rows, col_block):
        per_tile, last_blk = tm // rows, t // rows - 1
        return (pl.BlockSpec((rows, D_MODEL),
                             lambda i: (jnp.maximum(i * per_tile - 1, 0), col_block)),
                pl.BlockSpec((rows, D_MODEL),
                             lambda i: (jnp.minimum((i + 1) * per_tile, last_blk), col_block)))

    def gate(col):
        def spec(col_block):
            return pl.BlockSpec((tm, GATE_W), lambda i: (i, col_block))
        return [spec(col // GATE_W + half) for half in range(D_MODEL // GATE_W)]

    wspec = _resident((D_MODEL, D_MODEL), lambda i: (0, 0))
    assert tm % POOL_SUB == 0
    band = _pool_band(POOL_SUB)
    u_blk = COL_U // D_MODEL
    return pl.pallas_call(
        functools.partial(_mixer_merge_kernel, tiles_per_seq=seq // tm),
        grid=(t // tm,),
        in_specs=[row(COL_A // D_MODEL),
                  row(u_blk), *halo(POOL_HALO, u_blk),
                  row(0),
                  *gate(COL_GA), *gate(COL_GP), *gate(COL_GT),
                  row(0),
                  _resident(band.shape, lambda i: (0, 0, 0)),
                  _resident((POOL_GROUPS, POOL_CG, POOL_CG), lambda i: (0, 0, 0)),
                  pl.BlockSpec((None, 1, D_MODEL), lambda i: (layer, 0, 0)),
                  wspec, wspec, wspec],
        out_specs=row(0),
        out_shape=jax.ShapeDtypeStruct((t, D_MODEL), F32),
        compiler_params=_params(56, ("parallel",)),
        name="mixer_merge",
    )(proj, proj, proj, proj, attn,
      proj, proj, proj, proj, proj, proj, x,
      band, w_pool, pool_scale, w_a_out, w_attn_out, w_o)


def _ffn_kernel(x_ref, g_ref, wgu_ref, wd_ref, gf_ref, o_ref, *, chunk, final, row_split):
    rows = x_ref.shape[0] // row_split
    for r0 in range(0, x_ref.shape[0], rows):
        x = x_ref[r0:r0 + rows, :]
        h = _rms(x, g_ref[...]).astype(BF16)
        acc = x
        for c0, n in _col_chunks(D_FF, chunk):
            gate = jnp.dot(h, wgu_ref[:, c0:c0 + n], preferred_element_type=F32)
            up = jnp.dot(h, wgu_ref[:, D_FF + c0:D_FF + c0 + n], preferred_element_type=F32)
            act = (jax.nn.silu(gate) * up).astype(BF16)
            acc = acc + jnp.dot(act, wd_ref[c0:c0 + n, :], preferred_element_type=F32)
        o_ref[r0:r0 + rows, :] = _rms(acc, gf_ref[...]) if final else acc


def _ffn(x, g_ffn, w_gu, w_down, g_final, layer, casts, *, final, tm=ROW_TILE, chunk=1536,
         row_split=2):
    t = x.shape[0]
    row = pl.BlockSpec((tm, D_MODEL), lambda i: (i, 0))
    return _call_with_casts(
        functools.partial(_ffn_kernel, chunk=chunk, final=final, row_split=row_split),
        casts,
        grid=(t // tm,),
        in_specs=[row,
                  pl.BlockSpec((None, 1, D_MODEL), lambda i: (layer, 0, 0)),
                  _resident((D_MODEL, 2 * D_FF), lambda i: (0, 0)),
                  _resident((D_FF, D_MODEL), lambda i: (0, 0)),
                  pl.BlockSpec((1, D_MODEL), lambda i: (0, 0))],
        out_spec=row,
        out_shape=jax.ShapeDtypeStruct((t, D_MODEL), F32),
        operands=(x, g_ffn, w_gu, w_down, g_final),
        compiler_params=_params(56, ("arbitrary",)),
        name="ffn",
    )


@jax.jit
def _trunk(x, w_in, conv_w, w_a_out, w_pool, pool_scale, w_attn_out, attn_sink, w_o,
           g_mix, g_ffn, w_gu, w_down, rel_bias, g_final):
    batch, seq, d = x.shape
    depth = w_in.shape[0]
    xf = x.reshape(batch * seq, d)

    conv_w3 = conv_w.reshape(depth, 3, d)
    pool_scale3 = pool_scale.reshape(depth, 1, d)
    g_mix3 = g_mix.reshape(depth, 1, d)
    g_ffn3 = g_ffn.reshape(depth, 1, d)
    g_final2 = g_final.reshape(1, d)
    w_pool2 = w_pool.reshape(depth, POOL_GROUPS * POOL_CG, POOL_CG)
    steps = (batch * seq) // ROW_TILE
    attn_steps = (batch * seq) // ATTN_TILE
    bias, w_in_l = _bias_table(rel_bias, [_Cast(w_in, 0, d // N_HEADS)])
    for l in range(depth):
        proj, wa_b, wt_b, wo_b, wp_b = _inproj(
            xf, g_mix3, conv_w3, w_in_l, l,
            [_Cast(w, l, d // steps) for w in (w_a_out, w_attn_out, w_o, w_pool2)], seq=seq)
        attn, wgu_b, wd_b = _attention(
            proj, bias, attn_sink, l,
            [_Cast(w_gu, l, d // attn_steps), _Cast(w_down, l, D_FF // W_DOWN_CAST_BLOCKS)],
            seq=seq)
        x1 = _mixer_merge(proj, attn, xf, wp_b.reshape(POOL_GROUPS, POOL_CG, POOL_CG),
                          pool_scale3, wa_b, wt_b, wo_b, l, seq=seq)
        last = l == depth - 1
        xf, *nxt = _ffn(x1, g_ffn3, wgu_b, wd_b, g_final2, l,
                        [] if last else [_Cast(w_in, l + 1, d // steps)], final=last)
        w_in_l = nxt[0] if nxt else None
    return xf.reshape(batch, seq, d)


def kernel(x, w_in, conv_w, w_a_out, w_pool, pool_scale, w_attn_out, attn_sink, w_o, g_mix,
           g_ffn, w_gu, w_down, rel_bias, g_final):
    return _trunk(x, w_in, conv_w, w_a_out, w_pool, pool_scale, w_attn_out, attn_sink, w_o,
                  g_mix, g_ffn, w_gu, w_down, rel_bias, g_final)
```

```python
import functools
import math

import jax
import jax.numpy as jnp
from jax import lax
from jax.experimental import pallas as pl
from jax.experimental.pallas import tpu as pltpu

F32 = jnp.float32
BF16 = jnp.bfloat16

D_MODEL = 1024
N_HEADS = 16
N_KV_HEADS = 4
HEAD_DIM = 64
GROUP = N_HEADS // N_KV_HEADS
WINDOW = 128
BLOCK = 128
N_BUCKETS = 32
MAX_DISTANCE = 128
POOL_GROUPS = 4
POOL_CG = D_MODEL // POOL_GROUPS
D_FF = 2816
EPS = 1e-6
NEG_INF = -1e30

W_COL_B, W_COL_C, W_COL_X, W_COL_REST = 0, 1024, 2048, 3072
IN_TOTAL = 8704
COL_A, COL_U, COL_Q = 0, 1024, 2048
COL_K, COL_V = 3072, 3328
COL_GA, COL_GP, COL_GT = 3584, 4608, 5632
PROJ_W = D_MODEL + IN_TOTAL - W_COL_REST
GATE_W = 512

LANES = 128
MXU_W = 256
ROW_TILE = 512
ATTN_TILE = 1024
W_DOWN_CAST_BLOCKS = 11
BF16_ROWS = 16
SUM_ROWS = BF16_ROWS
CONV_HALO = BF16_ROWS
POOL_HALO = 64
POOL_SUB = 128
POOL_WINDOWS = (2, 4, 8, 16)
MIB = 1024 * 1024

VAR_MID, VAR_FIRST, VAR_LAST = 0, 1, 2


def _params(vmem_mib, sem):
    return pltpu.CompilerParams(dimension_semantics=sem, vmem_limit_bytes=vmem_mib * MIB)


def _resident(shape, index_map):
    return pl.BlockSpec(shape, index_map, pipeline_mode=pl.Buffered(1))


def _rms(x, g):
    ms = jnp.mean(x * x, axis=-1, keepdims=True)
    return (x * lax.rsqrt(ms + EPS)) * g


class _Cast:
    def __init__(self, weights, layer, rows):
        _, r, c = weights.shape
        assert r % rows == 0 and rows % BF16_ROWS == 0
        self.weights, self.layer, self.rows, self.r, self.c = weights, layer, rows, r, c

    def specs(self, n_steps):
        last = self.r // self.rows - 1
        assert last < n_steps
        layer = self.layer
        return (pl.BlockSpec((None, self.rows, self.c),
                             lambda i: (layer, jnp.minimum(i, last), 0)),
                pl.BlockSpec((self.rows, self.c), lambda i: (jnp.minimum(i, last), 0)),
                jax.ShapeDtypeStruct((self.r, self.c), BF16))


def _with_casts(body, n_in, n_casts):
    def kernel(*refs):
        own_in = refs[:n_in]
        cast_src = refs[n_in:n_in + n_casts]
        out = refs[n_in + n_casts]
        cast_dst = refs[n_in + n_casts + 1:n_in + 2 * n_casts + 1]
        scratch = refs[n_in + 2 * n_casts + 1:]
        for src, dst in zip(cast_src, cast_dst):
            dst[...] = src[...].astype(BF16)
        body(*own_in, out, *scratch)
    return kernel


def _call_with_casts(body, casts, *, grid, in_specs, out_spec, out_shape, operands, **kwargs):
    n_steps = grid[0]
    cast_specs = [c.specs(n_steps) for c in casts]
    outs = pl.pallas_call(
        _with_casts(body, len(in_specs), len(casts)),
        grid=grid,
        in_specs=list(in_specs) + [s[0] for s in cast_specs],
        out_specs=[out_spec] + [s[1] for s in cast_specs],
        out_shape=[out_shape] + [s[2] for s in cast_specs],
        **kwargs,
    )(*operands, *[c.weights for c in casts])
    return tuple(outs)


def _col_chunks(total, width):
    assert total % MXU_W == 0 and width % MXU_W == 0
    return [(c, min(width, total - c)) for c in range(0, total, width)]


def _shift(x, k):
    n = x.shape[0]
    return pltpu.roll(x, k % n, 0)


def _inproj_kernel(x_ref, xp_ref, xn_ref, g_ref, cw_ref, w_ref, o_ref,
                   *, chunk, conv_chunk, tiles_per_seq):
    tm = x_ref.shape[0]
    halo = xp_ref.shape[0]
    n = lax.rem(pl.program_id(0), tiles_per_seq)
    g = g_ref[...]
    h = _rms(x_ref[...], g).astype(BF16)
    h_prev = jnp.where(n == 0, 0.0, _rms(xp_ref[...], g)).astype(BF16)
    h_next = jnp.where(n == tiles_per_seq - 1, 0.0, _rms(xn_ref[...], g)).astype(BF16)
    h_ext = jnp.concatenate([h_prev, h, h_next], axis=0)

    for c0, nc in _col_chunks(D_MODEL, conv_chunk):
        def proj(lhs, col):
            return jnp.dot(lhs, w_ref[:, col + c0:col + c0 + nc], preferred_element_type=F32)

        u = proj(h_ext, W_COL_C) * proj(h_ext, W_COL_X)
        w = cw_ref[:, c0:c0 + nc]
        y = _shift(u, 1) * w[0:1, :] + u * w[1:2, :] + _shift(u, -1) * w[2:3, :]
        o_ref[:, COL_A + c0:COL_A + c0 + nc] = (
            proj(h, W_COL_B) * y[halo:halo + tm, :]).astype(BF16)

    for c0, nc in _col_chunks(IN_TOTAL - W_COL_REST, chunk):
        o_ref[:, COL_U + c0:COL_U + c0 + nc] = jnp.dot(
            h, w_ref[:, W_COL_REST + c0:W_COL_REST + c0 + nc],
            preferred_element_type=F32).astype(BF16)


def _inproj(x, g, conv_w, w, layer, casts, *, seq, tm=ROW_TILE, chunk=1024, conv_chunk=256):
    t = x.shape[0]
    assert seq % tm == 0 and tm % CONV_HALO == 0
    per_tile, last_blk = tm // CONV_HALO, t // CONV_HALO - 1
    return _call_with_casts(
        functools.partial(_inproj_kernel, chunk=chunk, conv_chunk=conv_chunk,
                          tiles_per_seq=seq // tm),
        casts,
        grid=(t // tm,),
        in_specs=[
            pl.BlockSpec((tm, D_MODEL), lambda i: (i, 0)),
            pl.BlockSpec((CONV_HALO, D_MODEL), lambda i: (jnp.maximum(i * per_tile - 1, 0), 0)),
            pl.BlockSpec((CONV_HALO, D_MODEL),
                         lambda i: (jnp.minimum((i + 1) * per_tile, last_blk), 0)),
            pl.BlockSpec((None, 1, D_MODEL), lambda i: (layer, 0, 0)),
            pl.BlockSpec((None, 3, D_MODEL), lambda i: (layer, 0, 0)),
            _resident((D_MODEL, IN_TOTAL), lambda i: (0, 0)),
        ],
        out_spec=pl.BlockSpec((tm, PROJ_W), lambda i: (i, 0)),
        out_shape=jax.ShapeDtypeStruct((t, PROJ_W), BF16),
        operands=(x, x, x, g, conv_w, w),
        compiler_params=_params(52, ("arbitrary",)),
        name="inproj",
    )


def _t5_bucket(rel):
    half = N_BUCKETS // 2
    max_exact = half // 2
    ret = jnp.where(rel > 0, half, 0)
    n = jnp.abs(rel)
    nf = jnp.maximum(n, 1).astype(jnp.float32)
    large = max_exact + (jnp.log(nf / max_exact) / math.log(MAX_DISTANCE / max_exact)
                         * (half - max_exact)).astype(jnp.int32)
    large = jnp.minimum(large, half - 1)
    return ret + jnp.where(n < max_exact, n, large)


def _bucket_map():
    kj = jnp.arange(3 * BLOCK)[:, None]
    qi = jnp.arange(BLOCK)[None, :]
    rel = kj - BLOCK - qi
    return jnp.where(jnp.abs(rel) <= WINDOW, _t5_bucket(rel), -1).astype(jnp.int32)


def _bias_kernel(relb_ref, bk_ref, o_ref):
    h = pl.program_id(0)
    bk = bk_ref[...]
    acc = jnp.full(bk.shape, NEG_INF, F32)
    for b in range(N_BUCKETS):
        acc = jnp.where(bk == b, relb_ref[b, h], acc)
    key = lax.broadcasted_iota(jnp.int32, bk.shape, 0)
    o_ref[VAR_MID] = acc
    o_ref[VAR_FIRST] = jnp.where(key < BLOCK, NEG_INF, acc)
    o_ref[VAR_LAST] = jnp.where(key >= 2 * BLOCK, NEG_INF, acc)


def _bias_table(rel_bias, casts):
    return _call_with_casts(
        _bias_kernel,
        casts,
        grid=(N_HEADS,),
        in_specs=[
            pl.BlockSpec(memory_space=pltpu.SMEM),
            pl.BlockSpec((3 * BLOCK, BLOCK), lambda h: (0, 0)),
        ],
        out_spec=pl.BlockSpec((3, None, 3 * BLOCK, BLOCK), lambda h: (0, h, 0, 0)),
        out_shape=jax.ShapeDtypeStruct((3, N_HEADS, 3 * BLOCK, BLOCK), F32),
        operands=(rel_bias, _bucket_map()),
        compiler_params=_params(16, ("arbitrary",)),
        name="bias_table",
    )


def _attn_kernel(sink_ref, q_ref, kc_ref, kp_ref, kn_ref, vc_ref, vp_ref, vn_ref, bias_ref,
                 o_ref, klo_ref, khi_ref, vt_ref, *, layer, tq, blocks_per_seq):
    i = pl.program_id(0)
    nqb = tq // BLOCK
    assert 2 * HEAD_DIM == LANES and blocks_per_seq >= 2

    for pair in range(N_KV_HEADS // 2):
        r = 0
        for part in (kp_ref, kc_ref, kn_ref):
            n = part.shape[0]
            t = pltpu.bitcast(part[:, pair * LANES:(pair + 1) * LANES], jnp.uint32)
            moved = pltpu.roll(t, HEAD_DIM, 1)
            low = lax.broadcasted_iota(jnp.int32, t.shape, 1) < HEAD_DIM
            zero = jnp.zeros_like(t)
            for a, lo, hi in ((2 * pair, jnp.where(low, t, zero), jnp.where(low, zero, moved)),
                              (2 * pair + 1, jnp.where(low, moved, zero),
                               jnp.where(low, zero, t))):
                klo_ref[a, r:r + n, :] = pltpu.bitcast(lo, BF16)
                khi_ref[a, r:r + n, :] = pltpu.bitcast(hi, BF16)
            r += n
        c = 0
        for part in (vp_ref, vc_ref, vn_ref):
            for b0 in range(0, part.shape[0], BLOCK):
                blk = part[b0:b0 + BLOCK, pair * LANES:(pair + 1) * LANES].astype(F32)
                blk_t = blk.T.astype(BF16)
                vt_ref[2 * pair, 0:HEAD_DIM, c:c + BLOCK] = blk_t[0:HEAD_DIM, :]
                vt_ref[2 * pair + 1, 0:HEAD_DIM, c:c + BLOCK] = blk_t[HEAD_DIM:, :]
                c += BLOCK
    for a in range(N_KV_HEADS):
        vt_ref[a, HEAD_DIM:, :] = jnp.ones((SUM_ROWS, vt_ref.shape[2]), BF16)

    nt_dims = (((1,), (1,)), ((), ()))
    q_scale = jnp.asarray(HEAD_DIM ** -0.5, BF16)

    def scores(qb, a):
        q0 = qb * BLOCK
        c0 = a * GROUP * HEAD_DIM
        qt = jnp.concatenate([q_ref[q0:q0 + BLOCK, c0:c0 + LANES],
                              q_ref[q0:q0 + BLOCK, c0 + LANES:c0 + 2 * LANES]], axis=0)
        qt = qt * q_scale
        win = slice(q0, q0 + 3 * BLOCK)
        return [lax.dot_general(k_ref[a, win, :], qt, nt_dims, preferred_element_type=F32)
                for k_ref in (klo_ref, khi_ref)]

    def softmax(qb, a, s_par):
        n = lax.rem(i * nqb + qb, blocks_per_seq)
        var = jnp.where(n == 0, VAR_FIRST, jnp.where(n == blocks_per_seq - 1, VAR_LAST, VAR_MID))
        pts, sinks = [], []
        for par in range(2):
            p_t, e_t = [], []
            for tile in range(2):
                h = a * GROUP + 2 * tile + par
                sink = sink_ref[layer, h]
                sg = s_par[par][:, tile * BLOCK:(tile + 1) * BLOCK] + bias_ref[var, h]
                m = jnp.maximum(jnp.max(sg, axis=0, keepdims=True), sink)
                p_t.append(jnp.exp(sg - m).astype(BF16))
                e_t.append(jnp.exp(sink - m))
            pts.append(jnp.concatenate(p_t, axis=1))
            sinks.append(jnp.concatenate(e_t, axis=1))
        return pts, sinks

    def values(qb, a, pts, sinks):
        q0 = qb * BLOCK
        c0 = a * GROUP * HEAD_DIM
        v_t = vt_ref[a, :, q0:q0 + 3 * BLOCK]
        halves = []
        for par in range(2):
            o_full = jnp.dot(v_t, pts[par], preferred_element_type=F32)
            denom = o_full[HEAD_DIM:HEAD_DIM + 1, :] + sinks[par]
            halves.append(o_full[0:HEAD_DIM, :] / denom)
        o_t = jnp.concatenate(halves, axis=0)
        for tile in range(2):
            o_ref[q0:q0 + BLOCK, c0 + tile * LANES:c0 + (tile + 1) * LANES] = (
                o_t[:, tile * BLOCK:(tile + 1) * BLOCK].T.astype(BF16))

    units = [(qb, a) for qb in range(nqb) for a in range(N_KV_HEADS)]
    s_next = scores(*units[0])
    sm_prev = None
    for t in range(len(units) + 1):
        s_cur = s_next
        if t + 1 < len(units):
            s_next = scores(*units[t + 1])
        sm_cur = softmax(*units[t], s_cur) if t < len(units) else None
        if t >= 1:
            values(*units[t - 1], *sm_prev)
        sm_prev = sm_cur


def _attention(proj, bias, sink, layer, casts, *, seq, tq=ATTN_TILE):
    t = proj.shape[0]
    assert seq % tq == 0 and tq % BLOCK == 0
    nb = tq // BLOCK
    last_blk = t // BLOCK - 1
    kvw = N_KV_HEADS * HEAD_DIM
    kcol, vcol = COL_K // kvw, COL_V // kvw
    prev_map = lambda c: (lambda i: (jnp.maximum(i * nb - 1, 0), c))
    next_map = lambda c: (lambda i: (jnp.minimum((i + 1) * nb, last_blk), c))
    k_scratch = pltpu.VMEM((N_KV_HEADS, tq + 2 * BLOCK, LANES), BF16)
    vt_scratch = pltpu.VMEM((N_KV_HEADS, HEAD_DIM + SUM_ROWS, tq + 2 * BLOCK), BF16)
    return _call_with_casts(
        functools.partial(_attn_kernel, layer=layer, tq=tq, blocks_per_seq=seq // BLOCK),
        casts,
        grid=(t // tq,),
        in_specs=[
            pl.BlockSpec(memory_space=pltpu.SMEM),
            pl.BlockSpec((tq, D_MODEL), lambda i: (i, COL_Q // D_MODEL)),
            pl.BlockSpec((tq, kvw), lambda i: (i, kcol)),
            pl.BlockSpec((BLOCK, kvw), prev_map(kcol)),
            pl.BlockSpec((BLOCK, kvw), next_map(kcol)),
            pl.BlockSpec((tq, kvw), lambda i: (i, vcol)),
            pl.BlockSpec((BLOCK, kvw), prev_map(vcol)),
            pl.BlockSpec((BLOCK, kvw), next_map(vcol)),
            _resident((3, N_HEADS, 3 * BLOCK, BLOCK), lambda i: (0, 0, 0, 0)),
        ],
        out_spec=pl.BlockSpec((tq, D_MODEL), lambda i: (i, 0)),
        out_shape=jax.ShapeDtypeStruct((t, D_MODEL), BF16),
        operands=(sink, proj, proj, proj, proj, proj, proj, proj, bias),
        scratch_shapes=[k_scratch, k_scratch, vt_scratch],
        compiler_params=_params(48, ("arbitrary",)),
        name="window_attn",
    )


def _pool_band(rows):
    t_i = jnp.arange(rows)[:, None]
    j_i = jnp.arange(rows + 2 * POOL_HALO)[None, :]
    d = j_i - POOL_HALO - t_i
    bands = [(d >= -(w // 2)) & (d <= w - 1 - w // 2) for w in POOL_WINDOWS]
    return jnp.stack(bands).astype(BF16)


def _mixer_merge_kernel(a_ref, u_ref, up_ref, un_ref, t_ref,
                        ga0_ref, ga1_ref, gp0_ref, gp1_ref, gt0_ref, gt1_ref, x_ref,
                        band_ref, wp_ref, ps_ref, wa_ref, wt_ref, wo_ref,
                        o_ref, *, tiles_per_seq):
    tm = a_ref.shape[0]
    seq = tm * tiles_per_seq
    n = lax.rem(pl.program_id(0), tiles_per_seq)
    first, last = n == 0, n == tiles_per_seq - 1
    cg = POOL_CG
    assert D_MODEL // cg == POOL_GROUPS

    def gate(lo_ref, hi_ref, rows):
        g = jnp.concatenate([lo_ref[rows, :], hi_ref[rows, :]], axis=1)
        return jax.nn.sigmoid(g.astype(F32))

    def window_sum(g):
        cols = slice(g * cg, (g + 1) * cg)
        prev = jnp.where(first, 0.0, up_ref[:, cols].astype(F32)).astype(BF16)
        nxt = jnp.where(last, 0.0, un_ref[:, cols].astype(F32)).astype(BF16)
        ctx = jnp.concatenate([prev, u_ref[:, cols], nxt], axis=0)
        sub = band_ref.shape[1]
        return jnp.concatenate(
            [jnp.dot(band_ref[g], ctx[r0:r0 + sub + 2 * POOL_HALO, :],
                     preferred_element_type=F32) for r0 in range(0, tm, sub)], axis=0)

    t_abs = n * tm + lax.broadcasted_iota(jnp.int32, (tm, 1), 0)

    def pooled(g, s):
        cols = slice(g * cg, (g + 1) * cg)
        win = POOL_WINDOWS[g]
        lo_off, hi_off = win // 2, win - 1 - win // 2
        cnt = jnp.minimum(t_abs + hi_off, seq - 1) - jnp.maximum(t_abs - lo_off, 0) + 1
        p = (s / cnt.astype(F32) - u_ref[:, cols].astype(F32)).astype(BF16)
        return jnp.dot(p, wp_ref[g], preferred_element_type=F32) * ps_ref[:, cols]

    sums = [window_sum(g) for g in range(POOL_GROUPS)]
    yp = jnp.concatenate([pooled(g, sums[g]) for g in range(POOL_GROUPS)], axis=1)
    half = tm // 2
    for r0 in (0, half):
        rows = slice(r0, r0 + half)
        ya = jnp.dot(a_ref[rows, :], wa_ref[...], preferred_element_type=F32)
        merged = gate(ga0_ref, ga1_ref, rows) * ya + gate(gp0_ref, gp1_ref, rows) * yp[rows, :]
        yt = jnp.dot(t_ref[rows, :], wt_ref[...], preferred_element_type=F32)
        merged = merged + gate(gt0_ref, gt1_ref, rows) * yt
        o_ref[rows, :] = x_ref[rows, :] + jnp.dot(merged.astype(BF16), wo_ref[...],
                                                  preferred_element_type=F32)


def _mixer_merge(proj, attn, x, w_pool, pool_scale, w_a_out, w_attn_out, w_o, layer,
                 *, seq, tm=ROW_TILE):
    t = x.shape[0]
    assert seq % tm == 0 and tm % POOL_HALO == 0

    def row(col_block):
        return pl.BlockSpec((tm, D_MODEL), lambda i: (i, col_block))

    def halo(rows, col_block):
        per_tile, last_blk = tm // rows, t // rows - 1
        return (pl.BlockSpec((rows, D_MODEL),
                             lambda i: (jnp.maximum(i * per_tile - 1, 0), col_block)),
                pl.BlockSpec((rows, D_MODEL),
                             lambda i: (jnp.minimum((i + 1) * per_tile, last_blk), col_block)))

    def gate(col):
        def spec(col_block):
            return pl.BlockSpec((tm, GATE_W), lambda i: (i, col_block))
        return [spec(col // GATE_W + half) for half in range(D_MODEL // GATE_W)]

    wspec = _resident((D_MODEL, D_MODEL), lambda i: (0, 0))
    assert tm % POOL_SUB == 0
    band = _pool_band(POOL_SUB)
    u_blk = COL_U // D_MODEL
    return pl.pallas_call(
        functools.partial(_mixer_merge_kernel, tiles_per_seq=seq // tm),
        grid=(t // tm,),
        in_specs=[row(COL_A // D_MODEL),
                  row(u_blk), *halo(POOL_HALO, u_blk),
                  row(0),
                  *gate(COL_GA), *gate(COL_GP), *gate(COL_GT),
                  row(0),
                  _resident(band.shape, lambda i: (0, 0, 0)),
                  _resident((POOL_GROUPS, POOL_CG, POOL_CG), lambda i: (0, 0, 0)),
                  pl.BlockSpec((None, 1, D_MODEL), lambda i: (layer, 0, 0)),
                  wspec, wspec, wspec],
        out_specs=row(0),
        out_shape=jax.ShapeDtypeStruct((t, D_MODEL), F32),
        compiler_params=_params(56, ("parallel",)),
        name="mixer_merge",
    )(proj, proj, proj, proj, attn,
      proj, proj, proj, proj, proj, proj, x,
      band, w_pool, pool_scale, w_a_out, w_attn_out, w_o)


def _ffn_kernel(x_ref, g_ref, wgu_ref, wd_ref, gf_ref, o_ref, *, chunk, final, row_split):
    rows = x_ref.shape[0] // row_split
    for r0 in range(0, x_ref.shape[0], rows):
        x = x_ref[r0:r0 + rows, :]
        h = _rms(x, g_ref[...]).astype(BF16)
        acc = x
        for c0, n in _col_chunks(D_FF, chunk):
            gate = jnp.dot(h, wgu_ref[:, c0:c0 + n], preferred_element_type=F32)
            up = jnp.dot(h, wgu_ref[:, D_FF + c0:D_FF + c0 + n], preferred_element_type=F32)
            act = (jax.nn.silu(gate) * up).astype(BF16)
            acc = acc + jnp.dot(act, wd_ref[c0:c0 + n, :], preferred_element_type=F32)
        o_ref[r0:r0 + rows, :] = _rms(acc, gf_ref[...]) if final else acc


def _ffn(x, g_ffn, w_gu, w_down, g_final, layer, casts, *, final, tm=ROW_TILE, chunk=1536,
         row_split=2):
    t = x.shape[0]
    row = pl.BlockSpec((tm, D_MODEL), lambda i: (i, 0))
    return _call_with_casts(
        functools.partial(_ffn_kernel, chunk=chunk, final=final, row_split=row_split),
        casts,
        grid=(t // tm,),
        in_specs=[row,
                  pl.BlockSpec((None, 1, D_MODEL), lambda i: (layer, 0, 0)),
                  _resident((D_MODEL, 2 * D_FF), lambda i: (0, 0)),
                  _resident((D_FF, D_MODEL), lambda i: (0, 0)),
                  pl.BlockSpec((1, D_MODEL), lambda i: (0, 0))],
        out_spec=row,
        out_shape=jax.ShapeDtypeStruct((t, D_MODEL), F32),
        operands=(x, g_ffn, w_gu, w_down, g_final),
        compiler_params=_params(56, ("arbitrary",)),
        name="ffn",
    )


@jax.jit
def _trunk(x, w_in, conv_w, w_a_out, w_pool, pool_scale, w_attn_out, attn_sink, w_o,
           g_mix, g_ffn, w_gu, w_down, rel_bias, g_final):
    batch, seq, d = x.shape
    depth = w_in.shape[0]
    xf = x.reshape(batch * seq, d)

    conv_w3 = conv_w.reshape(depth, 3, d)
    pool_scale3 = pool_scale.reshape(depth, 1, d)
    g_mix3 = g_mix.reshape(depth, 1, d)
    g_ffn3 = g_ffn.reshape(depth, 1, d)
    g_final2 = g_final.reshape(1, d)
    w_pool2 = w_pool.reshape(depth, POOL_GROUPS * POOL_CG, POOL_CG)
    steps = (batch * seq) // ROW_TILE
    attn_steps = (batch * seq) // ATTN_TILE
    bias, w_in_l = _bias_table(rel_bias, [_Cast(w_in, 0, d // N_HEADS)])
    for l in range(depth):
        proj, wa_b, wt_b, wo_b, wp_b = _inproj(
            xf, g_mix3, conv_w3, w_in_l, l,
            [_Cast(w, l, d // steps) for w in (w_a_out, w_attn_out, w_o, w_pool2)], seq=seq)
        attn, wgu_b, wd_b = _attention(
            proj, bias, attn_sink, l,
            [_Cast(w_gu, l, d // attn_steps), _Cast(w_down, l, D_FF // W_DOWN_CAST_BLOCKS)],
            seq=seq)
        x1 = _mixer_merge(proj, attn, xf, wp_b.reshape(POOL_GROUPS, POOL_CG, POOL_CG),
                          pool_scale3, wa_b, wt_b, wo_b, l, seq=seq)
        last = l == depth - 1
        xf, *nxt = _ffn(x1, g_ffn3, wgu_b, wd_b, g_final2, l,
                        [] if last else [_Cast(w_in, l + 1, d // steps)], final=last)
        w_in_l = nxt[0] if nxt else None
    return xf.reshape(batch, seq, d)


def kernel(x, w_in, conv_w, w_a_out, w_pool, pool_scale, w_attn_out, attn_sink, w_o, g_mix,
           g_ffn, w_gu, w_down, rel_bias, g_final):
    return _trunk(x, w_in, conv_w, w_a_out, w_pool, pool_scale, w_attn_out, attn_sink, w_o,
                  g_mix, g_ffn, w_gu, w_down, rel_bias, g_final)
```
